```python
import jax, jax.numpy as jnp
from jax import lax
import numpy as np

D_MODEL = 1024
BATCH = 32
SEQ = 2048
DEPTH = 2
DEC_BATCH = 8
DEC_SEQ = 16
PAST_LEN = 4096

CHUNK = 64
N_MIXERS = 2
N_ATTN = (DEPTH + 1) // 2
N_GMLP = DEPTH // 2
N_HEADS = 8
HEAD_DIM = D_MODEL // N_HEADS
N_KV_HEADS = 2
ROPE_FRAC_DIV = 4
ROPE_THETA = 500000.0
N_IDX_HEADS = 8
IDX_DIM = 64
TOPK_MAX = 256
Q_BLOCK = 128
O_Q = N_HEADS * HEAD_DIM
O_K = O_Q + N_KV_HEADS * HEAD_DIM
O_V = O_K + N_KV_HEADS * HEAD_DIM
O_QI = O_V + N_IDX_HEADS * IDX_DIM
O_KI = O_QI + IDX_DIM
ATTN_PROJ = O_KI + N_IDX_HEADS
D_GM = D_MODEL
GM_GROUPS = 8
GM_CHUNK = 128
D_FF = 2816
CONV_W = 3
PLE_DIM = 256
EPS = 1e-6
NEG_INF = -1e30

kernel_name = 'hybrid_dsa_gmlp_streaming_step'


def rms_norm(x, g):
    xf = x.astype(jnp.float32)
    y = xf * lax.rsqrt(jnp.mean(xf * xf, axis=-1, keepdims=True) + EPS)
    return (y * g.astype(jnp.float32)).astype(x.dtype)


def partial_rope(x, pos):
    d = x.shape[-1]
    r = d // ROPE_FRAC_DIV
    half = r // 2
    inv = ROPE_THETA ** (-jnp.arange(half, dtype=jnp.float32) * (2.0 / r))
    ang = pos.astype(jnp.float32)[:, None] * inv[None, :]
    cos = jnp.cos(ang)[None, :, None, :]
    sin = jnp.sin(ang)[None, :, None, :]
    xf = x.astype(jnp.float32)
    x1 = xf[..., :half]
    x2 = xf[..., half:r]
    out = jnp.concatenate([x1 * cos - x2 * sin, x2 * cos + x1 * sin, xf[..., r:]], axis=-1)
    return out.astype(x.dtype)


def attn_project(h, w_in, q_g, k_g, pos):
    B, T, _ = h.shape
    z = h @ w_in
    q = z[..., :O_Q].reshape(B, T, N_HEADS, HEAD_DIM)
    k = z[..., O_Q:O_K].reshape(B, T, N_KV_HEADS, HEAD_DIM)
    v = z[..., O_K:O_V].reshape(B, T, N_KV_HEADS, HEAD_DIM)
    qi = z[..., O_V:O_QI].reshape(B, T, N_IDX_HEADS, IDX_DIM)
    ki = z[..., O_QI:O_KI].reshape(B, T, 1, IDX_DIM)
    wi = z[..., O_KI:]
    q = partial_rope(rms_norm(q, q_g), pos)
    k = partial_rope(rms_norm(k, k_g), pos)
    qi = partial_rope(qi, pos)
    ki = partial_rope(ki, pos)[:, :, 0]
    return q, k, v, qi, ki, wi


def sparse_attend(q, qi, wi, q_pos, k, v, ki, k_sel):
    B, Tq = q.shape[0], q.shape[1]
    L = k.shape[1]
    s = jnp.einsum('bqhd,bkd->bqhk', qi.astype(jnp.float32), ki.astype(jnp.float32)) * (IDX_DIM ** -0.5)
    score = jnp.einsum('bqhk,bqh->bqk', jax.nn.relu(s), wi.astype(jnp.float32) * (N_IDX_HEADS ** -0.5))
    limit = (q_pos // CHUNK + 1) * CHUNK
    admissible = jnp.arange(L, dtype=jnp.int32)[None, :] < limit[:, None]
    score = jnp.where(admissible[None], score, NEG_INF)
    _, idx = lax.top_k(score, k_sel)
    valid = idx < limit[None, :, None]
    kg = jax.vmap(lambda kb, ib: kb[ib])(k, idx)
    vg = jax.vmap(lambda vb, ib: vb[ib])(v, idx)
    qg = q.reshape(B, Tq, N_KV_HEADS, N_HEADS // N_KV_HEADS, HEAD_DIM)
    logits = jnp.einsum('bqgrd,bqkgd->bqgrk', qg.astype(jnp.float32), kg.astype(jnp.float32)) * (HEAD_DIM ** -0.5)
    logits = jnp.where(valid[:, :, None, None, :], logits, NEG_INF)
    p = jax.nn.softmax(logits, axis=-1).astype(v.dtype)
    o = jnp.einsum('bqgrk,bqkgd->bqgrd', p, vg)
    return o.reshape(B, Tq, N_HEADS * HEAD_DIM)


def attn_mixer_prompt(h, w_in, q_g, k_g, w_out):
    B, T, _ = h.shape
    pos = jnp.arange(T, dtype=jnp.int32)
    q, k, v, qi, ki, wi = attn_project(h, w_in, q_g, k_g, pos)
    k_sel = min(TOPK_MAX, T // 4)
    nb = T // Q_BLOCK

    def to_blocks(a):
        return jnp.moveaxis(a.reshape((B, nb, Q_BLOCK) + a.shape[2:]), 1, 0)

    def one_block(args):
        qb, qib, wib, pb = args
        return sparse_attend(qb, qib, wib, pb, k, v, ki, k_sel)

    o = lax.map(one_block, (to_blocks(q), to_blocks(qi), to_blocks(wi), pos.reshape(nb, Q_BLOCK)))
    o = jnp.moveaxis(o, 0, 1).reshape(B, T, N_HEADS * HEAD_DIM)
    return o @ w_out, k, v, ki


def attn_mixer_sample(h, ck, cv, cki, w_in, q_g, k_g, w_out):
    B, T, _ = h.shape
    P = ck.shape[1]
    pos = P + jnp.arange(T, dtype=jnp.int32)
    q, k, v, qi, ki, wi = attn_project(h, w_in, q_g, k_g, pos)
    k_all = jnp.concatenate([ck, k], axis=1)
    v_all = jnp.concatenate([cv, v], axis=1)
    ki_all = jnp.concatenate([cki, ki], axis=1)
    k_sel = min(TOPK_MAX, (P + T) // 4)
    o = sparse_attend(q, qi, wi, pos, k_all, v_all, ki_all, k_sel)
    return o @ w_out, k, v, ki


def gmlp_mixer(h, w_in, v_g, w_s, b_s, w_out):
    B, T, _ = h.shape
    z = jax.nn.gelu(h @ w_in)
    u = z[..., :D_GM]
    v = rms_norm(z[..., D_GM:], v_g)
    Tc = min(T, GM_CHUNK)
    nc = T // Tc
    mask = jnp.tril(jnp.ones((Tc, Tc), dtype=w_s.dtype))
    ws = w_s[:, :Tc, :Tc] * mask
    vb = v.reshape(B, nc, Tc, GM_GROUPS, D_GM // GM_GROUPS)
    mixed = jnp.einsum('gts,bnsgd->bntgd', ws, vb) + b_s[:, :Tc].T[None, None, :, :, None]
    s = u * mixed.reshape(B, T, D_GM)
    return s @ w_out, v


def conv_ffn(h, past, w_up, conv_w, conv_b, w_down):
    T = h.shape[1]
    a = h @ w_up
    full = jnp.concatenate([past, a], axis=1)
    c = conv_b
    for j in range(CONV_W):
        c = c + conv_w[j] * full[:, j:j + T]
    g = c[..., :D_FF]
    up = c[..., D_FF:]
    return (jax.nn.silu(g) * up) @ w_down, full[:, T:]


def ple_add(h, p, g, w_gate, w_proj):
    return h + jax.nn.sigmoid(rms_norm(h, g) @ w_gate) * (p @ w_proj)


def setup_inputs(seed: int = 0) -> dict:
    key = jax.random.key(seed)
    ks = jax.random.split(key, 32)
    f32 = jnp.float32

    def nrm(k, shape, scale):
        return jax.random.normal(k, shape, f32) * scale

    def gain(k, shape):
        return 1.0 + 0.05 * jax.random.normal(k, shape, f32)

    return {
        'x_prompt': nrm(ks[0], (BATCH, SEQ, D_MODEL), 1.0),
        'x_sample': nrm(ks[1], (DEC_BATCH, DEC_SEQ, D_MODEL), 1.0),
        'cache_k': nrm(ks[2], (N_ATTN, DEC_BATCH, PAST_LEN, N_KV_HEADS, HEAD_DIM), 1.0),
        'cache_v': nrm(ks[3], (N_ATTN, DEC_BATCH, PAST_LEN, N_KV_HEADS, HEAD_DIM), 1.0),
        'cache_kidx': nrm(ks[4], (N_ATTN, DEC_BATCH, PAST_LEN, IDX_DIM), 1.0),
        'state_ffn_conv': nrm(ks[5], (DEPTH, DEC_BATCH, CONV_W - 1, 2 * D_FF), 1.0),
        'p_prompt': nrm(ks[6], (DEPTH, BATCH, SEQ, PLE_DIM), 1.0),
        'p_sample': nrm(ks[7], (DEPTH, DEC_BATCH, DEC_SEQ, PLE_DIM), 1.0),
        'norm_mix': gain(ks[8], (DEPTH, D_MODEL)),
        'attn_w_in': nrm(ks[9], (N_ATTN, D_MODEL, ATTN_PROJ), D_MODEL ** -0.5),
        'attn_q_norm': gain(ks[10], (N_ATTN, HEAD_DIM)),
        'attn_k_norm': gain(ks[11], (N_ATTN, HEAD_DIM)),
        'attn_w_out': nrm(ks[12], (N_ATTN, N_HEADS * HEAD_DIM, D_MODEL), (N_HEADS * HEAD_DIM) ** -0.5),
        'gmlp_w_in': nrm(ks[13], (N_GMLP, D_MODEL, 2 * D_GM), D_MODEL ** -0.5),
        'gmlp_v_norm': gain(ks[14], (N_GMLP, D_GM)),
        'gmlp_w_spatial': nrm(ks[15], (N_GMLP, GM_GROUPS, GM_CHUNK, GM_CHUNK), GM_CHUNK ** -0.5),
        'gmlp_b_spatial': gain(ks[16], (N_GMLP, GM_GROUPS, GM_CHUNK)),
        'gmlp_w_out': nrm(ks[17], (N_GMLP, D_GM, D_MODEL), D_GM ** -0.5),
        'norm_ffn': gain(ks[18], (DEPTH, D_MODEL)),
        'ffn_w_up': nrm(ks[19], (DEPTH, D_MODEL, 2 * D_FF), D_MODEL ** -0.5),
        'ffn_conv_w': nrm(ks[20], (DEPTH, CONV_W, 2 * D_FF), CONV_W ** -0.5),
        'ffn_conv_b': nrm(ks[21], (DEPTH, 2 * D_FF), 0.01),
        'ffn_w_down': nrm(ks[22], (DEPTH, D_FF, D_MODEL), D_FF ** -0.5),
        'norm_ple': gain(ks[23], (DEPTH, D_MODEL)),
        'ple_w_gate': nrm(ks[24], (DEPTH, D_MODEL, D_MODEL), D_MODEL ** -0.5),
        'ple_w_proj': nrm(ks[25], (DEPTH, PLE_DIM, D_MODEL), PLE_DIM ** -0.5),
    }


def reference(x_prompt, x_sample, cache_k, cache_v, cache_kidx, state_ffn_conv, p_prompt, p_sample,
              norm_mix, attn_w_in, attn_q_norm, attn_k_norm, attn_w_out,
              gmlp_w_in, gmlp_v_norm, gmlp_w_spatial, gmlp_b_spatial, gmlp_w_out,
              norm_ffn, ffn_w_up, ffn_conv_w, ffn_conv_b, ffn_w_down,
              norm_ple, ple_w_gate, ple_w_proj):
    yp = x_prompt
    ys = x_sample
    kp_l, vp_l, kip_l, ks_l, vs_l, kis_l, gv_l, cp_l, cs_l = [], [], [], [], [], [], [], [], []
    for i in range(DEPTH):
        j = i // N_MIXERS
        hp = rms_norm(yp, norm_mix[i])
        hs = rms_norm(ys, norm_mix[i])
        if i % N_MIXERS == 0:
            op, kp, vp, kip = attn_mixer_prompt(hp, attn_w_in[j], attn_q_norm[j], attn_k_norm[j], attn_w_out[j])
            os_, ks_, vs_, kis = attn_mixer_sample(hs, cache_k[j], cache_v[j], cache_kidx[j],
                                                   attn_w_in[j], attn_q_norm[j], attn_k_norm[j], attn_w_out[j])
            kp_l.append(kp); vp_l.append(vp); kip_l.append(kip)
            ks_l.append(ks_); vs_l.append(vs_); kis_l.append(kis)
        else:
            op, _ = gmlp_mixer(hp, gmlp_w_in[j], gmlp_v_norm[j], gmlp_w_spatial[j], gmlp_b_spatial[j], gmlp_w_out[j])
            os_, gv = gmlp_mixer(hs, gmlp_w_in[j], gmlp_v_norm[j], gmlp_w_spatial[j], gmlp_b_spatial[j], gmlp_w_out[j])
            gv_l.append(gv)
        yp = yp + op
        ys = ys + os_
        zeros_past = jnp.zeros((yp.shape[0], CONV_W - 1, 2 * D_FF), dtype=yp.dtype)
        fp, cp = conv_ffn(rms_norm(yp, norm_ffn[i]), zeros_past, ffn_w_up[i], ffn_conv_w[i], ffn_conv_b[i], ffn_w_down[i])
        fs, cs = conv_ffn(rms_norm(ys, norm_ffn[i]), state_ffn_conv[i], ffn_w_up[i], ffn_conv_w[i], ffn_conv_b[i], ffn_w_down[i])
        cp_l.append(cp); cs_l.append(cs)
        yp = ple_add(yp + fp, p_prompt[i], norm_ple[i], ple_w_gate[i], ple_w_proj[i])
        ys = ple_add(ys + fs, p_sample[i], norm_ple[i], ple_w_gate[i], ple_w_proj[i])
    new_k_prompt = jnp.stack(kp_l, 0)
    new_v_prompt = jnp.stack(vp_l, 0)
    new_kidx_prompt = jnp.stack(kip_l, 0)
    new_k_sample = jnp.stack(ks_l, 0)
    new_v_sample = jnp.stack(vs_l, 0)
    new_kidx_sample = jnp.stack(kis_l, 0)
    new_gmlp_v_sample = jnp.stack(gv_l, 0)
    new_conv_prompt = jnp.stack(cp_l, 0)
    new_conv_sample = jnp.stack(cs_l, 0)
    return (yp, ys, new_k_prompt, new_v_prompt, new_kidx_prompt, new_k_sample, new_v_sample,
            new_kidx_sample, new_gmlp_v_sample, new_conv_prompt, new_conv_sample)
```

```python
import functools

import numpy as np
import jax
import jax.numpy as jnp
from jax import lax
from jax.experimental import pallas as pl
from jax.experimental.pallas import tpu as pltpu

F32 = jnp.float32
BF16 = jnp.bfloat16
I32 = jnp.int32

D_MODEL = 1024
N_HEADS = 8
HEAD_DIM = 128
N_KV_HEADS = 2
N_IDX_HEADS = 8
IDX_DIM = 64
CHUNK = 64
TOPK_MAX = 256
ROPE_THETA = 500000.0
ROPE_FRAC_DIV = 4
GM_GROUPS = 8
GM_CHUNK = 128
D_FF = 2816
CONV_W = 3
EPS = 1e-6

O_Q = N_HEADS * HEAD_DIM
O_K = O_Q + N_KV_HEADS * HEAD_DIM
O_V = O_K + N_KV_HEADS * HEAD_DIM
O_QI = O_V + N_IDX_HEADS * IDX_DIM
O_KI = O_QI + IDX_DIM
ATTN_PROJ = O_KI + N_IDX_HEADS

LANES = 128
ATTN_PROJ_PAD = 2176
INT_MIN = -2147483648
MASK_BIAS = -1e30
KEY_TILE = 256
COUNT_ROWS = 64
Q_ROWS = 128
VMEM_LIMIT = 56 * 1024 * 1024


def _cparams(sem):
    return pltpu.CompilerParams(dimension_semantics=sem, vmem_limit_bytes=VMEM_LIMIT)


def _nt_dot(a, b):
    return lax.dot_general(a, b, (((1,), (1,)), ((), ())), preferred_element_type=F32)


def _rms(x):
    return x * lax.rsqrt(jnp.mean(x * x, axis=-1, keepdims=True) + EPS)


def _rope_tables(pos, head_w):
    rot = head_w // ROPE_FRAC_DIV
    half = rot // 2
    inv = ROPE_THETA ** (-jnp.arange(half, dtype=F32) * (2.0 / rot))
    ang = pos.astype(F32)[:, None] * inv[None, :]
    cos, sin = jnp.cos(ang), jnp.sin(ang)
    j = np.arange(LANES) % head_w
    idx = j % half
    first = jnp.asarray(j < half)[None, :]
    second = jnp.asarray((j >= half) & (j < rot))[None, :]
    cos_t = jnp.where(first | second, cos[:, idx], 1.0)
    sin_t = jnp.where(first, -sin[:, idx], jnp.where(second, sin[:, idx], 0.0))
    return cos_t.astype(F32), sin_t.astype(F32)


def _attn_proj_body(x_ref, gm_ref, w_ref, qg_ref, kg_ref, cq_ref, sq_ref, ci_ref, si_ref,
                    q_ref, kf_ref, vf_ref, kif_ref, kb_ref, vb_ref, kic_ref, qic_ref, wt_ref,
                    z_ref, *, tm):
    x = x_ref[...]
    h = _rms(x) * gm_ref[...]
    z_ref[...] = jnp.dot(h.astype(BF16), w_ref[...], preferred_element_type=F32)

    rb = LANES
    lane = lax.broadcasted_iota(I32, (rb, LANES), 1)
    first_q = lane < (HEAD_DIM // ROPE_FRAC_DIV // 2)
    first_i = (lane & (IDX_DIM - 1)) < (IDX_DIM // ROPE_FRAC_DIV // 2)
    low = lane < IDX_DIM
    qg = qg_ref[...]
    kg = kg_ref[...]
    w_scale = (N_IDX_HEADS ** -0.5) * (IDX_DIM ** -0.5)

    def rope(y, c, s, first, sh):
        partner = jnp.where(first, pltpu.roll(y, LANES - sh, 1), pltpu.roll(y, sh, 1))
        return y * c + partner * s

    def split(y):
        hi = y.astype(BF16).astype(F32)
        return hi, y - hi

    for r in range(tm // rb):
        rows = slice(r * rb, (r + 1) * rb)
        cq, sq = cq_ref[rows, :], sq_ref[rows, :]
        ci, si = ci_ref[rows, :], si_ref[rows, :]
        for hd in range(N_HEADS):
            cols = slice(hd * HEAD_DIM, (hd + 1) * HEAD_DIM)
            y = rope(_rms(z_ref[rows, cols]) * qg, cq, sq, first_q, 16)
            q_ref[rows, cols] = y.astype(BF16)
        for g in range(N_KV_HEADS):
            cols = slice(g * HEAD_DIM, (g + 1) * HEAD_DIM)
            zk = z_ref[rows, O_Q + g * HEAD_DIM:O_Q + (g + 1) * HEAD_DIM]
            y = rope(_rms(zk) * kg, cq, sq, first_q, 16)
            kf_ref[rows, cols] = y
            kb_ref[rows, cols] = y.astype(BF16)
            v = z_ref[rows, O_K + g * HEAD_DIM:O_K + (g + 1) * HEAD_DIM]
            vf_ref[rows, cols] = v
            vb_ref[rows, cols] = v.astype(BF16)
        for t in range(N_IDX_HEADS * IDX_DIM // LANES):
            y = rope(z_ref[rows, O_V + t * LANES:O_V + (t + 1) * LANES], ci, si, first_i, 8)
            hi, lo = split(y)
            a = jnp.where(low, hi, pltpu.roll(lo, IDX_DIM, 1))
            b = jnp.where(low, pltpu.roll(hi, IDX_DIM, 1), lo)
            qic_ref[rows, (2 * t) * LANES:(2 * t + 1) * LANES] = a.astype(BF16)
            qic_ref[rows, (2 * t + 1) * LANES:(2 * t + 2) * LANES] = b.astype(BF16)
        zt = z_ref[rows, O_QI:O_QI + LANES]
        y = rope(zt, ci, si, first_i, 8)
        kif_ref[rows, :] = y[:, :IDX_DIM]
        hi, lo = split(y)
        kic_ref[rows, 0:LANES] = jnp.where(low, hi, pltpu.roll(hi, IDX_DIM, 1)).astype(BF16)
        kic_ref[rows, LANES:2 * LANES] = jnp.where(low, lo, pltpu.roll(lo, IDX_DIM, 1)).astype(BF16)
        wt_ref[:, rows] = zt.T[IDX_DIM:IDX_DIM + N_IDX_HEADS, :] * w_scale


def _attn_proj(x, g_mix, w_in, q_g, k_g, tabs, *, tm, n_tab):
    n = x.shape[0]
    cq, sq, ci, si = tabs
    row = lambda i: (i, 0)
    tab = lambda i: (i % n_tab, 0)
    const = lambda i: (0, 0)
    out_shape = (
        jax.ShapeDtypeStruct((n, O_Q), BF16),
        jax.ShapeDtypeStruct((n, N_KV_HEADS * HEAD_DIM), F32),
        jax.ShapeDtypeStruct((n, N_KV_HEADS * HEAD_DIM), F32),
        jax.ShapeDtypeStruct((n, IDX_DIM), F32),
        jax.ShapeDtypeStruct((n, N_KV_HEADS * HEAD_DIM), BF16),
        jax.ShapeDtypeStruct((n, N_KV_HEADS * HEAD_DIM), BF16),
        jax.ShapeDtypeStruct((n, 2 * LANES), BF16),
        jax.ShapeDtypeStruct((n, N_IDX_HEADS * LANES), BF16),
        jax.ShapeDtypeStruct((N_IDX_HEADS, n), F32),
    )
    widths = (O_Q, 256, 256, IDX_DIM, 256, 256, 2 * LANES, N_IDX_HEADS * LANES)
    out_specs = tuple(pl.BlockSpec((tm, w), row) for w in widths) + (
        pl.BlockSpec((N_IDX_HEADS, tm), lambda i: (0, i)),)
    return pl.pallas_call(
        functools.partial(_attn_proj_body, tm=tm),
        out_shape=out_shape,
        grid=(n // tm,),
        in_specs=[
            pl.BlockSpec((tm, D_MODEL), row),
            pl.BlockSpec((1, D_MODEL), const),
            pl.BlockSpec((D_MODEL, ATTN_PROJ_PAD), const),
            pl.BlockSpec((1, HEAD_DIM), const),
            pl.BlockSpec((1, HEAD_DIM), const),
            pl.BlockSpec((tm, LANES), tab),
            pl.BlockSpec((tm, LANES), tab),
            pl.BlockSpec((tm, LANES), tab),
            pl.BlockSpec((tm, LANES), tab),
        ],
        out_specs=out_specs,
        scratch_shapes=[pltpu.VMEM((tm, ATTN_PROJ_PAD), F32)],
        compiler_params=_cparams(("parallel",)),
        name="attn_proj",
    )(x, g_mix, w_in, q_g, k_g, cq, sq, ci, si)


def _sparse_attn_body(q_ref, qi_ref, wt_ref, ki_ref, k_ref, v_ref, o_ref,
                      keys_ref, mask_ref, bias_ref, *, tq, pos0, l_true, n_valid, k_sel):
    j = pl.program_id(1)
    q0 = pos0 + j * tq
    lane_q = lax.broadcasted_iota(I32, (1, tq), 1)
    qpos = q0 + lane_q
    limit = jnp.minimum((lax.shift_right_logical(qpos, 6) + 1) * CHUNK, l_true)
    lim_max = jnp.minimum((lax.shift_right_logical(q0 + tq - 1, 6) + 1) * CHUNK, l_true)
    n_kt = lax.shift_right_logical(lim_max + KEY_TILE - 1, 8)

    sub = KEY_TILE // 2
    row_iota = lax.broadcasted_iota(I32, (sub, tq), 0)

    def score_tile(t, carry):
        r0 = pl.multiple_of(t * sub, sub)
        kt = ki_ref[0, pl.ds(r0, sub), :]
        acc = jnp.zeros((sub, tq), F32)
        for h in range(N_IDX_HEADS):
            qh = qi_ref[0, :, h * LANES:(h + 1) * LANES]
            s = _nt_dot(kt, jnp.concatenate([qh, qh], axis=1))
            acc = acc + wt_ref[h:h + 1, :] * jnp.maximum(s, 0.0)
        bits = pltpu.bitcast(acc, I32)
        key = bits ^ (lax.shift_right_arithmetic(bits, 31) & 0x7FFFFFFF)
        key = jnp.where(r0 + row_iota < limit, key, INT_MIN)
        keys_ref[pl.ds(r0, sub), :] = key
        return carry

    lax.fori_loop(0, n_kt * 2, score_tile, 0)

    def count(pred):
        def body(t, acc):
            for u in range(KEY_TILE // COUNT_ROWS):
                r0 = pl.multiple_of(t * KEY_TILE + u * COUNT_ROWS, COUNT_ROWS)
                acc = acc + jnp.where(pred(keys_ref[pl.ds(r0, COUNT_ROWS), :], r0), 1.0, 0.0)
            return acc
        acc = lax.fori_loop(0, n_kt, body, jnp.zeros((COUNT_ROWS, tq), F32))
        return jnp.sum(acc, axis=0, keepdims=True)

    kf = float(k_sel)

    def search(p, ans):
        cand = ans + lax.shift_left(jnp.int32(1), 31 - p)
        cnt = count(lambda c, r0: c >= cand)
        return jnp.where(cnt >= kf, cand, ans)

    thr = lax.fori_loop(0, 32, search, jnp.full((1, tq), INT_MIN, I32))
    thr_c = jnp.maximum(thr, INT_MIN + 1)
    n_ge = count(lambda c, r0: c >= thr_c)

    def write_mask(sel):
        def body(t, carry):
            r0 = pl.multiple_of(t * KEY_TILE, KEY_TILE)
            m = jnp.where(sel(keys_ref[pl.ds(r0, KEY_TILE), :], r0), 1.0, 0.0)
            mask_ref[pl.ds(r0, KEY_TILE), :] = m.astype(BF16)
            return carry
        lax.fori_loop(0, n_kt, body, 0)

    write_mask(lambda c, r0: c >= thr_c)

    excess = jnp.where((n_ge > kf) & (lane_q < n_valid), 1.0, 0.0)

    @pl.when(jnp.max(excess) > 0.0)
    def _():
        need = kf - count(lambda c, r0: c > thr)
        idx_iota = lax.broadcasted_iota(I32, (COUNT_ROWS, tq), 0)
        last = jnp.zeros((1, tq), I32)
        for bit in range(13, -1, -1):
            cand = last | (1 << bit)
            cnt = count(lambda c, r0: (c == thr) & (r0 + idx_iota < cand))
            last = jnp.where(cnt < need, cand, last)
        tile_iota = lax.broadcasted_iota(I32, (KEY_TILE, tq), 0)
        write_mask(lambda c, r0: (c > thr) | ((c == thr) & (c > INT_MIN) & (r0 + tile_iota <= last)))

    eye = jnp.where(lax.broadcasted_iota(I32, (tq, tq), 0) == lax.broadcasted_iota(I32, (tq, tq), 1),
                    1.0, 0.0).astype(BF16)

    def bias_tile(t, carry):
        r0 = pl.multiple_of(t * KEY_TILE, KEY_TILE)
        sel = _nt_dot(eye, mask_ref[pl.ds(r0, KEY_TILE), :])
        bias_ref[t] = (sel - 1.0) * (-MASK_BIAS)
        return carry

    lax.fori_loop(0, n_kt, bias_tile, 0)

    c2 = (HEAD_DIM ** -0.5) * 1.4426950408889634
    for h in range(N_HEADS):
        g = h // (N_HEADS // N_KV_HEADS)
        kcols = slice(g * HEAD_DIM, (g + 1) * HEAD_DIM)
        hcols = slice(h * HEAD_DIM, (h + 1) * HEAD_DIM)
        for rb in range(tq // Q_ROWS):
            rows = slice(rb * Q_ROWS, (rb + 1) * Q_ROWS)
            qh = q_ref[0, rows, hcols]

            def step(t, carry):
                m, l, acc = carry
                r0 = pl.multiple_of(t * KEY_TILE, KEY_TILE)
                s = _nt_dot(qh, k_ref[0, pl.ds(r0, KEY_TILE), kcols]) + bias_ref[t, rows, :]
                m_new = jnp.maximum(m, jnp.max(s, axis=-1, keepdims=True))
                p = jnp.exp2((s - m_new) * c2)
                alpha = jnp.exp2((m - m_new) * c2)
                l = l * alpha + jnp.sum(p, axis=-1, keepdims=True)
                acc = acc * alpha + jnp.dot(p.astype(BF16), v_ref[0, pl.ds(r0, KEY_TILE), kcols],
                                            preferred_element_type=F32)
                return m_new, l, acc

            init = (jnp.full((Q_ROWS, 1), MASK_BIAS, F32), jnp.zeros((Q_ROWS, 1), F32),
                    jnp.zeros((Q_ROWS, HEAD_DIM), F32))
            _, l, acc = lax.fori_loop(0, n_kt, step, init)
            o_ref[0, rows, hcols] = (acc / l).astype(BF16)


def _sparse_attn(q, qi, wt, ki, k, v, *, tq, pos0, l_true, n_valid):
    b, t_q, _ = q.shape
    lp = k.shape[1]
    nq = t_q // tq
    k_sel = min(TOPK_MAX, l_true // 4)
    body = functools.partial(_sparse_attn_body, tq=tq, pos0=pos0, l_true=l_true,
                             n_valid=n_valid, k_sel=k_sel)
    qspec = pl.BlockSpec((1, tq, O_Q), lambda i, j: (i, j, 0))
    kspec = pl.BlockSpec((1, lp, 2 * LANES), lambda i, j: (i, 0, 0))
    return pl.pallas_call(
        body,
        out_shape=jax.ShapeDtypeStruct((b, t_q, O_Q), BF16),
        grid=(b, nq),
        in_specs=[qspec, qspec,
                  pl.BlockSpec((N_IDX_HEADS, tq), lambda i, j: (0, i * nq + j)),
                  kspec, kspec, kspec],
        out_specs=qspec,
        scratch_shapes=[pltpu.VMEM((lp, tq), I32), pltpu.VMEM((lp, tq), BF16),
                        pltpu.VMEM((lp // KEY_TILE, tq, KEY_TILE), F32)],
        compiler_params=_cparams(("parallel", "arbitrary")),
        name="sparse_attn",
    )(q, qi, wt, ki, k, v)


def _out_proj_body(x_ref, o_ref, w_ref, y_ref):
    y_ref[...] = x_ref[...] + jnp.dot(o_ref[...], w_ref[...], preferred_element_type=F32)


def _out_proj(x, o, w, *, tm):
    n = x.shape[0]
    row = lambda i: (i, 0)
    return pl.pallas_call(
        _out_proj_body,
        out_shape=jax.ShapeDtypeStruct((n, D_MODEL), F32),
        grid=(n // tm,),
        in_specs=[pl.BlockSpec((tm, D_MODEL), row), pl.BlockSpec((tm, O_Q), row),
                  pl.BlockSpec((O_Q, D_MODEL), lambda i: (0, 0))],
        out_specs=pl.BlockSpec((tm, D_MODEL), row),
        compiler_params=_cparams(("parallel",)),
        name="out_proj",
    )(x, o, w)


def _gmlp_body(x_ref, gm_ref, w_in_ref, vg_ref, ws_ref, bs_ref, w_out_ref, *rest, tm, tc, emit_v):
    if emit_v:
        y_ref, v_ref, z_ref, s_ref = rest
    else:
        y_ref, z_ref, s_ref = rest
    x = x_ref[...]
    h = _rms(x) * gm_ref[...]
    z_ref[...] = jax.nn.gelu(jnp.dot(h.astype(BF16), w_in_ref[...], preferred_element_type=F32))
    v = _rms(z_ref[:, D_MODEL:]) * vg_ref[...]
    if emit_v:
        v_ref[...] = v
    z_ref[:, D_MODEL:] = v
    gw = D_MODEL // GM_GROUPS
    r_i = lax.broadcasted_iota(I32, (GM_CHUNK, GM_CHUNK), 0)
    c_i = lax.broadcasted_iota(I32, (GM_CHUNK, GM_CHUNK), 1)
    keep = (r_i // tc == c_i // tc) & (c_i <= r_i)
    for g in range(GM_GROUPS):
        ws = jnp.where(keep, ws_ref[g], 0.0).astype(BF16)
        bias = bs_ref[g]
        cols = slice(g * gw, (g + 1) * gw)
        for ch in range(tm // GM_CHUNK):
            rows = slice(ch * GM_CHUNK, (ch + 1) * GM_CHUNK)
            vb = z_ref[rows, D_MODEL + g * gw:D_MODEL + (g + 1) * gw].astype(BF16)
            mixed = jnp.dot(ws, vb, preferred_element_type=F32) + bias
            s_ref[rows, cols] = (z_ref[rows, cols] * mixed).astype(BF16)
    y_ref[...] = x + jnp.dot(s_ref[...], w_out_ref[...], preferred_element_type=F32)


def _gmlp(x, g_mix, w_in, v_g, ws, bs, w_out, *, tm, tc, emit_v):
    n = x.shape[0]
    row = lambda i: (i, 0)
    const2 = lambda i: (0, 0)
    out_shape = [jax.ShapeDtypeStruct((n, D_MODEL), F32)]
    out_specs = [pl.BlockSpec((tm, D_MODEL), row)]
    if emit_v:
        out_shape.append(jax.ShapeDtypeStruct((n, D_MODEL), F32))
        out_specs.append(pl.BlockSpec((tm, D_MODEL), row))
    res = pl.pallas_call(
        functools.partial(_gmlp_body, tm=tm, tc=tc, emit_v=emit_v),
        out_shape=tuple(out_shape),
        grid=(n // tm,),
        in_specs=[pl.BlockSpec((tm, D_MODEL), row), pl.BlockSpec((1, D_MODEL), const2),
                  pl.BlockSpec((D_MODEL, 2 * D_MODEL), const2), pl.BlockSpec((1, D_MODEL), const2),
                  pl.BlockSpec((GM_GROUPS, GM_CHUNK, GM_CHUNK), lambda i: (0, 0, 0)),
                  pl.BlockSpec((GM_GROUPS, GM_CHUNK, GM_CHUNK), lambda i: (0, 0, 0)),
                  pl.BlockSpec((D_MODEL, D_MODEL), const2)],
        out_specs=tuple(out_specs),
        scratch_shapes=[pltpu.VMEM((tm, 2 * D_MODEL), F32), pltpu.VMEM((tm, D_MODEL), BF16)],
        compiler_params=_cparams(("parallel",)),
        name="gmlp",
    )(x, g_mix, w_in, v_g, ws, bs, w_out)
    return res if emit_v else (res[0], None)


HALO = 8


def _ffn_body(*refs, tm, nt, fc, has_past):
    if has_past:
        (x_ref, gn_ref, wg_ref, wu_ref, cwg_ref, cwu_ref, cbg_ref, cbu_ref, wd_ref, pg_ref, pu_ref,
         y_ref, sg_ref, su_ref, hb_ref, acc_ref, ag_ref, au_ref, carg_ref, caru_ref) = refs
    else:
        (x_ref, gn_ref, wg_ref, wu_ref, cwg_ref, cwu_ref, cbg_ref, cbu_ref, wd_ref,
         y_ref, sg_ref, su_ref, hb_ref, acc_ref, ag_ref, au_ref, carg_ref, caru_ref) = refs
        pg_ref = pu_ref = None
    i = pl.program_id(0)
    c = pl.program_id(1)
    t = lax.rem(i, nt)

    @pl.when(c == 0)
    def _():
        hb_ref[...] = (_rms(x_ref[...]) * gn_ref[...]).astype(BF16)
        acc_ref[...] = jnp.zeros_like(acc_ref)

    def branch(w_ref, cw_ref, cb_ref, a_ref, car_ref, past_ref, s_ref):
        a_ref[HALO:HALO + tm, :] = jnp.dot(hb_ref[...], w_ref[...], preferred_element_type=F32)

        @pl.when(t == 0)
        def _():
            if past_ref is None:
                a_ref[HALO - 2:HALO, :] = jnp.zeros((CONV_W - 1, fc), F32)
            else:
                a_ref[HALO - 2:HALO, :] = past_ref[0]

        @pl.when(t > 0)
        def _():
            a_ref[HALO - 2:HALO, :] = car_ref[c, 0:CONV_W - 1, :]

        tail = a_ref[HALO + tm - 2:HALO + tm, :]
        car_ref[c, 0:CONV_W - 1, :] = tail
        s_ref[0] = tail
        cw = cw_ref[...]
        return (cb_ref[...] + cw[2:3, :] * a_ref[HALO:HALO + tm, :]
                + cw[1:2, :] * a_ref[HALO - 1:HALO - 1 + tm, :]
                + cw[0:1, :] * a_ref[HALO - 2:HALO - 2 + tm, :])

    gate = branch(wg_ref, cwg_ref, cbg_ref, ag_ref, carg_ref, pg_ref, sg_ref)
    up = branch(wu_ref, cwu_ref, cbu_ref, au_ref, caru_ref, pu_ref, su_ref)
    s = (gate * jax.nn.sigmoid(gate) * up).astype(BF16)
    acc_ref[...] += jnp.dot(s, wd_ref[...], preferred_element_type=F32)

    @pl.when(c == pl.num_programs(1) - 1)
    def _():
        y_ref[...] = x_ref[...] + acc_ref[...]


def _conv_ffn(x, g_norm, w_up, conv_w, conv_b, w_down, past, *, nb, tm, fc):
    n = x.shape[0]
    nt = n // nb // tm
    nc = D_FF // fc
    has_past = past is not None
    row = lambda i, c: (i, 0)
    in_specs = [
        pl.BlockSpec((tm, D_MODEL), row),
        pl.BlockSpec((1, D_MODEL), lambda i, c: (0, 0)),
        pl.BlockSpec((D_MODEL, fc), lambda i, c: (0, c)),
        pl.BlockSpec((D_MODEL, fc), lambda i, c: (0, nc + c)),
        pl.BlockSpec((CONV_W, fc), lambda i, c: (0, c)),
        pl.BlockSpec((CONV_W, fc), lambda i, c: (0, nc + c)),
        pl.BlockSpec((1, fc), lambda i, c: (0, c)),
        pl.BlockSpec((1, fc), lambda i, c: (0, nc + c)),
        pl.BlockSpec((fc, D_MODEL), lambda i, c: (c, 0)),
    ]
    args = [x, g_norm, w_up, w_up, conv_w, conv_w, conv_b, conv_b, w_down]
    if has_past:
        in_specs += [pl.BlockSpec((1, CONV_W - 1, fc), lambda i, c: (i // nt, 0, c)),
                     pl.BlockSpec((1, CONV_W - 1, fc), lambda i, c: (i // nt, 0, nc + c))]
        args += [past, past]
    state = jax.ShapeDtypeStruct((n // tm, CONV_W - 1, D_FF), F32)
    y, sg, su = pl.pallas_call(
        functools.partial(_ffn_body, tm=tm, nt=nt, fc=fc, has_past=has_past),
        out_shape=(jax.ShapeDtypeStruct((n, D_MODEL), F32), state, state),
        grid=(n // tm, nc),
        in_specs=in_specs,
        out_specs=(pl.BlockSpec((tm, D_MODEL), row),
                   pl.BlockSpec((1, CONV_W - 1, fc), lambda i, c: (i, 0, c)),
                   pl.BlockSpec((1, CONV_W - 1, fc), lambda i, c: (i, 0, c))),
        scratch_shapes=[pltpu.VMEM((tm, D_MODEL), BF16), pltpu.VMEM((tm, D_MODEL), F32),
                        pltpu.VMEM((tm + HALO, fc), F32), pltpu.VMEM((tm + HALO, fc), F32),
                        pltpu.VMEM((nc, HALO, fc), F32), pltpu.VMEM((nc, HALO, fc), F32)],
        compiler_params=_cparams(("arbitrary", "arbitrary")),
        name="conv_ffn",
    )(*args)
    tails = jnp.concatenate([sg, su], axis=-1).reshape(nb, nt, CONV_W - 1, 2 * D_FF)
    return y, tails[:, nt - 1]


def _ple_body(x_ref, p_ref, g_ref, wg_ref, wp_ref, y_ref):
    x = x_ref[...]
    h = (_rms(x) * g_ref[...]).astype(BF16)
    gate = jax.nn.sigmoid(jnp.dot(h, wg_ref[...], preferred_element_type=F32))
    proj = jnp.dot(p_ref[...].astype(BF16), wp_ref[...], preferred_element_type=F32)
    y_ref[...] = x + gate * proj


def _ple(x, p, g, w_gate, w_proj, *, tm):
    n = x.shape[0]
    pd = p.shape[1]
    row = lambda i: (i, 0)
    const = lambda i: (0, 0)
    return pl.pallas_call(
        _ple_body,
        out_shape=jax.ShapeDtypeStruct((n, D_MODEL), F32),
        grid=(n // tm,),
        in_specs=[pl.BlockSpec((tm, D_MODEL), row), pl.BlockSpec((tm, pd), row),
                  pl.BlockSpec((1, D_MODEL), const), pl.BlockSpec((D_MODEL, D_MODEL), const),
                  pl.BlockSpec((pd, D_MODEL), const)],
        out_specs=pl.BlockSpec((tm, D_MODEL), row),
        compiler_params=_cparams(("parallel",)),
        name="ple",
    )(x, p, g, w_gate, w_proj)


def _split_hi_lo_keys(ki):
    hi = ki.astype(BF16)
    lo = (ki - hi.astype(F32)).astype(BF16)
    return jnp.concatenate([hi, hi, lo, lo], axis=-1)


def _pad_rows(a, rows):
    return jnp.pad(a, ((0, 0), (0, rows - a.shape[1]), (0, 0)))


def _attn_layer(x, g_mix, w_in, q_g, k_g, w_out, cache, *, tm, tq):
    b, t, _ = x.shape
    n = b * t
    past_len = 0 if cache is None else cache[0].shape[1]
    pos = past_len + jnp.arange(t, dtype=I32)
    reps = tm // t if tm > t else 1
    tabs = tuple(jnp.tile(a, (reps, 1)) for a in _rope_tables(pos, HEAD_DIM) + _rope_tables(pos, IDX_DIM))
    n_tab = max(t // tm, 1)
    w_pad = jnp.pad(w_in, ((0, 0), (0, ATTN_PROJ_PAD - ATTN_PROJ))).astype(BF16)
    q, kf, vf, kif, kb, vb, kic, qic, wt = _attn_proj(
        x.reshape(n, D_MODEL), g_mix[None, :], w_pad, q_g[None, :], k_g[None, :], tabs, tm=tm, n_tab=n_tab)
    kvw = N_KV_HEADS * HEAD_DIM
    if cache is None:
        o = _sparse_attn(q.reshape(b, t, O_Q), qic.reshape(b, t, O_Q), wt,
                         kic.reshape(b, t, kvw), kb.reshape(b, t, kvw), vb.reshape(b, t, kvw),
                         tq=tq, pos0=0, l_true=t, n_valid=tq)
    else:
        ck, cv, cki = cache
        l_true = past_len + t
        lp = -(-l_true // KEY_TILE) * KEY_TILE
        k_all = _pad_rows(jnp.concatenate([ck.reshape(b, past_len, kvw).astype(BF16), kb.reshape(b, t, kvw)], 1), lp)
        v_all = _pad_rows(jnp.concatenate([cv.reshape(b, past_len, kvw).astype(BF16), vb.reshape(b, t, kvw)], 1), lp)
        ki_all = _pad_rows(jnp.concatenate([_split_hi_lo_keys(cki), kic.reshape(b, t, kvw)], 1), lp)
        q_p = _pad_rows(q.reshape(b, t, O_Q), tq)
        qi_p = _pad_rows(qic.reshape(b, t, O_Q), tq)
        wt_p = jnp.pad(wt.reshape(N_IDX_HEADS, b, t), ((0, 0), (0, 0), (0, tq - t))).reshape(N_IDX_HEADS, b * tq)
        o = _sparse_attn(q_p, qi_p, wt_p, ki_all, k_all, v_all,
                         tq=tq, pos0=past_len, l_true=l_true, n_valid=t)[:, :t]
    y = _out_proj(x.reshape(n, D_MODEL), o.reshape(n, O_Q), w_out.astype(BF16), tm=tm)
    return (y.reshape(b, t, D_MODEL), kf.reshape(b, t, N_KV_HEADS, HEAD_DIM),
            vf.reshape(b, t, N_KV_HEADS, HEAD_DIM), kif.reshape(b, t, IDX_DIM))


def _gmlp_layer(x, g_mix, w_in, v_g, w_s, b_s, w_out, *, tm, emit_v):
    b, t, _ = x.shape
    n = b * t
    tc = min(t, GM_CHUNK)
    reps = GM_CHUNK // tc
    ws = jnp.tile(w_s[:, :tc, :tc], (1, reps, reps))
    bs = jnp.broadcast_to(jnp.tile(b_s[:, :tc], (1, reps))[:, :, None], (GM_GROUPS, GM_CHUNK, GM_CHUNK))
    y, v = _gmlp(x.reshape(n, D_MODEL), g_mix[None, :], w_in.astype(BF16), v_g[None, :], ws, bs,
                 w_out.astype(BF16), tm=tm, tc=tc, emit_v=emit_v)
    return y.reshape(b, t, D_MODEL), (v.reshape(b, t, D_MODEL) if emit_v else None)


def _ffn_ple_layer(x, p, past, g_ffn, w_up, conv_w, conv_b, w_down, g_ple, w_gate, w_proj, *, tm, tm_ffn, fc):
    b, t, _ = x.shape
    n = b * t
    y, state = _conv_ffn(x.reshape(n, D_MODEL), g_ffn[None, :], w_up.astype(BF16), conv_w, conv_b[None, :],
                         w_down.astype(BF16), past, nb=b, tm=tm_ffn, fc=fc)
    y = _ple(y, p.reshape(n, p.shape[-1]), g_ple[None, :], w_gate.astype(BF16), w_proj.astype(BF16), tm=tm)
    return y.reshape(b, t, D_MODEL), state


def kernel(x_prompt, x_sample, cache_k, cache_v, cache_kidx, state_ffn_conv, p_prompt, p_sample,
           norm_mix, attn_w_in, attn_q_norm, attn_k_norm, attn_w_out,
           gmlp_w_in, gmlp_v_norm, gmlp_w_spatial, gmlp_b_spatial, gmlp_w_out,
           norm_ffn, ffn_w_up, ffn_conv_w, ffn_conv_b, ffn_w_down,
           norm_ple, ple_w_gate, ple_w_proj):
    depth = norm_mix.shape[0]
    t_s = x_sample.shape[1]
    n_s = x_sample.shape[0] * t_s
    yp, ys = x_prompt, x_sample
    kp, vp, kip, ks, vs, kis, gvs, cps, css = [], [], [], [], [], [], [], [], []
    for i in range(depth):
        j = i // 2
        if i % 2 == 0:
            yp, k, v, ki = _attn_layer(yp, norm_mix[i], attn_w_in[j], attn_q_norm[j], attn_k_norm[j],
                                       attn_w_out[j], None, tm=512, tq=256)
            kp.append(k); vp.append(v); kip.append(ki)
            ys, k, v, ki = _attn_layer(ys, norm_mix[i], attn_w_in[j], attn_q_norm[j], attn_k_norm[j],
                                       attn_w_out[j], (cache_k[j], cache_v[j], cache_kidx[j]), tm=n_s, tq=128)
            ks.append(k); vs.append(v); kis.append(ki)
        else:
            yp, _ = _gmlp_layer(yp, norm_mix[i], gmlp_w_in[j], gmlp_v_norm[j], gmlp_w_spatial[j],
                                gmlp_b_spatial[j], gmlp_w_out[j], tm=512, emit_v=False)
            ys, gv = _gmlp_layer(ys, norm_mix[i], gmlp_w_in[j], gmlp_v_norm[j], gmlp_w_spatial[j],
                                 gmlp_b_spatial[j], gmlp_w_out[j], tm=n_s, emit_v=True)
            gvs.append(gv)
        ffn = (norm_ffn[i], ffn_w_up[i], ffn_conv_w[i], ffn_conv_b[i], ffn_w_down[i],
               norm_ple[i], ple_w_gate[i], ple_w_proj[i])
        yp, cp = _ffn_ple_layer(yp, p_prompt[i], None, *ffn, tm=512, tm_ffn=512, fc=1408)
        ys, cs = _ffn_ple_layer(ys, p_sample[i], state_ffn_conv[i], *ffn, tm=n_s, tm_ffn=t_s, fc=1408)
        cps.append(cp); css.append(cs)
    return (yp, ys, jnp.stack(kp, 0), jnp.stack(vp, 0), jnp.stack(kip, 0),
            jnp.stack(ks, 0), jnp.stack(vs, 0), jnp.stack(kis, 0), jnp.stack(gvs, 0),
            jnp.stack(cps, 0), jnp.stack(css, 0))
```

```python
import functools

import numpy as np
import jax
import jax.numpy as jnp
from jax import lax
from jax.experimental import pallas as pl
from jax.experimental.pallas import tpu as pltpu

F32 = jnp.float32
BF16 = jnp.bfloat16
I32 = jnp.int32

D_MODEL = 1024
N_HEADS = 8
HEAD_DIM = 128
N_KV_HEADS = 2
N_IDX_HEADS = 8
IDX_DIM = 64
CHUNK = 64
TOPK_MAX = 256
ROPE_THETA = 500000.0
ROPE_FRAC_DIV = 4
GM_GROUPS = 8
GM_CHUNK = 128
D_FF = 2816
CONV_W = 3
EPS = 1e-6

O_Q = N_HEADS * HEAD_DIM
O_K = O_Q + N_KV_HEADS * HEAD_DIM
O_V = O_K + N_KV_HEADS * HEAD_DIM
O_QI = O_V + N_IDX_HEADS * IDX_DIM
O_KI = O_QI + IDX_DIM
ATTN_PROJ = O_KI + N_IDX_HEADS

LANES = 128
ATTN_PROJ_PAD = 2176
INT_MIN = -2147483648
MASK_BIAS = -1e30
KEY_TILE = 256
COUNT_ROWS = 64
Q_ROWS = 128
VMEM_LIMIT = 56 * 1024 * 1024


def _cparams(sem):
    return pltpu.CompilerParams(dimension_semantics=sem, vmem_limit_bytes=VMEM_LIMIT)


def _nt_dot(a, b):
    return lax.dot_general(a, b, (((1,), (1,)), ((), ())), preferred_element_type=F32)


def _rms(x):
    return x * lax.rsqrt(jnp.mean(x * x, axis=-1, keepdims=True) + EPS)


def _rope_tables(pos, head_w):
    rot = head_w // ROPE_FRAC_DIV
    half = rot // 2
    inv = ROPE_THETA ** (-jnp.arange(half, dtype=F32) * (2.0 / rot))
    ang = pos.astype(F32)[:, None] * inv[None, :]
    cos, sin = jnp.cos(ang), jnp.sin(ang)
    j = np.arange(LANES) % head_w
    idx = j % half
    first = jnp.asarray(j < half)[None, :]
    second = jnp.asarray((j >= half) & (j < rot))[None, :]
    cos_t = jnp.where(first | second, cos[:, idx], 1.0)
    sin_t = jnp.where(first, -sin[:, idx], jnp.where(second, sin[:, idx], 0.0))
    return cos_t.astype(F32), sin_t.astype(F32)


def _attn_proj_body(x_ref, gm_ref, w_ref, qg_ref, kg_ref, cq_ref, sq_ref, ci_ref, si_ref,
                    q_ref, kf_ref, vf_ref, kif_ref, kt_ref, vb_ref, kic_ref, qit_ref, wt_ref,
                    z_ref, *, tm):
    x = x_ref[...]
    h = _rms(x) * gm_ref[...]
    z_ref[...] = jnp.dot(h.astype(BF16), w_ref[...], preferred_element_type=F32)

    rb = LANES
    lane = lax.broadcasted_iota(I32, (rb, LANES), 1)
    first_q = lane < (HEAD_DIM // ROPE_FRAC_DIV // 2)
    first_i = (lane & (IDX_DIM - 1)) < (IDX_DIM // ROPE_FRAC_DIV // 2)
    low = lane < IDX_DIM
    qg = qg_ref[...]
    kg = kg_ref[...]
    w_scale = (N_IDX_HEADS ** -0.5) * (IDX_DIM ** -0.5)

    def rope(y, c, s, first, sh):
        partner = jnp.where(first, pltpu.roll(y, LANES - sh, 1), pltpu.roll(y, sh, 1))
        return y * c + partner * s

    def split(y):
        hi = y.astype(BF16).astype(F32)
        return hi, y - hi

    for r in range(tm // rb):
        rows = slice(r * rb, (r + 1) * rb)
        cq, sq = cq_ref[rows, :], sq_ref[rows, :]
        ci, si = ci_ref[rows, :], si_ref[rows, :]
        for hd in range(N_HEADS):
            cols = slice(hd * HEAD_DIM, (hd + 1) * HEAD_DIM)
            y = rope(_rms(z_ref[rows, cols]) * qg, cq, sq, first_q, 16)
            q_ref[rows, cols] = y.astype(BF16)
        for g in range(N_KV_HEADS):
            cols = slice(g * HEAD_DIM, (g + 1) * HEAD_DIM)
            zk = z_ref[rows, O_Q + g * HEAD_DIM:O_Q + (g + 1) * HEAD_DIM]
            y = rope(_rms(zk) * kg, cq, sq, first_q, 16)
            kf_ref[rows, cols] = y
            kt_ref[r, cols, :] = y.T.astype(BF16)
            v = z_ref[rows, O_K + g * HEAD_DIM:O_K + (g + 1) * HEAD_DIM]
            vf_ref[rows, cols] = v
            vb_ref[rows, cols] = v.astype(BF16)
        for t in range(N_IDX_HEADS * IDX_DIM // LANES):
            y = rope(z_ref[rows, O_V + t * LANES:O_V + (t + 1) * LANES], ci, si, first_i, 8)
            hi, lo = split(y)
            a = jnp.where(low, hi, pltpu.roll(lo, IDX_DIM, 1))
            b = jnp.where(low, pltpu.roll(hi, IDX_DIM, 1), lo)
            qit_ref[r, (2 * t) * LANES:(2 * t + 1) * LANES, :] = a.T.astype(BF16)
            qit_ref[r, (2 * t + 1) * LANES:(2 * t + 2) * LANES, :] = b.T.astype(BF16)
        zt = z_ref[rows, O_QI:O_QI + LANES]
        y = rope(zt, ci, si, first_i, 8)
        kif_ref[rows, :] = y[:, :IDX_DIM]
        hi, lo = split(y)
        kic_ref[rows, 0:LANES] = jnp.where(low, hi, pltpu.roll(hi, IDX_DIM, 1)).astype(BF16)
        kic_ref[rows, LANES:2 * LANES] = jnp.where(low, lo, pltpu.roll(lo, IDX_DIM, 1)).astype(BF16)
        wt_ref[:, rows] = zt.T[IDX_DIM:IDX_DIM + N_IDX_HEADS, :] * w_scale


def _attn_proj(x, g_mix, w_in, q_g, k_g, tabs, *, tm, n_tab):
    n = x.shape[0]
    cq, sq, ci, si = tabs
    row = lambda i: (i, 0)
    tab = lambda i: (i % n_tab, 0)
    const = lambda i: (0, 0)
    out_shape = (
        jax.ShapeDtypeStruct((n, O_Q), BF16),
        jax.ShapeDtypeStruct((n, N_KV_HEADS * HEAD_DIM), F32),
        jax.ShapeDtypeStruct((n, N_KV_HEADS * HEAD_DIM), F32),
        jax.ShapeDtypeStruct((n, IDX_DIM), F32),
        jax.ShapeDtypeStruct((n // LANES, N_KV_HEADS * HEAD_DIM, LANES), BF16),
        jax.ShapeDtypeStruct((n, N_KV_HEADS * HEAD_DIM), BF16),
        jax.ShapeDtypeStruct((n, 2 * LANES), BF16),
        jax.ShapeDtypeStruct((n // LANES, N_IDX_HEADS * LANES, LANES), BF16),
        jax.ShapeDtypeStruct((N_IDX_HEADS, n), F32),
    )
    tile3 = lambda i: (i, 0, 0)
    rspec = lambda w: pl.BlockSpec((tm, w), row)
    out_specs = (rspec(O_Q), rspec(256), rspec(256), rspec(IDX_DIM),
                 pl.BlockSpec((tm // LANES, N_KV_HEADS * HEAD_DIM, LANES), tile3),
                 rspec(256), rspec(2 * LANES),
                 pl.BlockSpec((tm // LANES, N_IDX_HEADS * LANES, LANES), tile3),
                 pl.BlockSpec((N_IDX_HEADS, tm), lambda i: (0, i)))
    return pl.pallas_call(
        functools.partial(_attn_proj_body, tm=tm),
        out_shape=out_shape,
        grid=(n // tm,),
        in_specs=[
            pl.BlockSpec((tm, D_MODEL), row),
            pl.BlockSpec((1, D_MODEL), const),
            pl.BlockSpec((D_MODEL, ATTN_PROJ_PAD), const),
            pl.BlockSpec((1, HEAD_DIM), const),
            pl.BlockSpec((1, HEAD_DIM), const),
            pl.BlockSpec((tm, LANES), tab),
            pl.BlockSpec((tm, LANES), tab),
            pl.BlockSpec((tm, LANES), tab),
            pl.BlockSpec((tm, LANES), tab),
        ],
        out_specs=out_specs,
        scratch_shapes=[pltpu.VMEM((tm, ATTN_PROJ_PAD), F32)],
        compiler_params=_cparams(("parallel",)),
        name="attn_proj",
    )(x, g_mix, w_in, q_g, k_g, cq, sq, ci, si)


def _sparse_attn_body(q_ref, qit_ref, wt_ref, ki_ref, kt_ref, v_ref, o_ref,
                      keys_ref, mask_ref, bias_ref, sacc_ref, qs_ref, s_ref, p_ref, alpha_ref, m_ref, acc_ref,
                      *, tq, pos0, l_true, n_valid, k_sel):
    j = pl.program_id(1)
    q0 = pos0 + j * tq
    nqt = tq // LANES
    lane_q = lax.broadcasted_iota(I32, (1, tq), 1)
    qpos = q0 + lane_q
    limit = jnp.minimum((lax.shift_right_logical(qpos, 6) + 1) * CHUNK, l_true)
    lim_max = jnp.minimum((lax.shift_right_logical(q0 + tq - 1, 6) + 1) * CHUNK, l_true)
    n_kt = lax.shift_right_logical(lim_max + KEY_TILE - 1, 8)

    row_iota = lax.broadcasted_iota(I32, (KEY_TILE, tq), 0)

    def score_tile(t, carry):
        r0 = pl.multiple_of(t * KEY_TILE, KEY_TILE)
        kt = ki_ref[0, pl.ds(r0, KEY_TILE), :]
        for h in range(N_IDX_HEADS):
            qh = jnp.concatenate([qit_ref[u, h * LANES:(h + 1) * LANES, :] for u in range(nqt)], axis=1)
            s = jnp.dot(kt, jnp.concatenate([qh, qh], axis=0), preferred_element_type=F32)
            term = wt_ref[h:h + 1, :] * jnp.maximum(s, 0.0)
            if h == 0:
                sacc_ref[...] = term
            elif h < N_IDX_HEADS - 1:
                sacc_ref[...] += term
            else:
                bits = pltpu.bitcast(sacc_ref[...] + term, I32)
                key = jnp.where(bits < 0, INT_MIN - bits, bits)
                keys_ref[pl.ds(r0, KEY_TILE), :] = jnp.where(r0 + row_iota < limit, key, INT_MIN)
        return carry

    lax.fori_loop(0, n_kt, score_tile, 0)

    def count(pred):
        def body(t, acc):
            for u in range(KEY_TILE // COUNT_ROWS):
                r0 = pl.multiple_of(t * KEY_TILE + u * COUNT_ROWS, COUNT_ROWS)
                acc = acc + jnp.where(pred(keys_ref[pl.ds(r0, COUNT_ROWS), :], r0), 1.0, 0.0)
            return acc
        acc = lax.fori_loop(0, n_kt, body, jnp.zeros((COUNT_ROWS, tq), F32))
        return jnp.sum(acc, axis=0, keepdims=True)

    kf = float(k_sel)

    def search(p, ans):
        cand = ans + lax.shift_left(jnp.int32(1), 31 - p)
        cnt = count(lambda c, r0: c >= cand)
        return jnp.where(cnt >= kf, cand, ans)

    thr = lax.fori_loop(0, 32, search, jnp.full((1, tq), INT_MIN, I32))
    thr_c = jnp.maximum(thr, INT_MIN + 1)
    n_ge = count(lambda c, r0: c >= thr_c)

    def write_mask(sel):
        def body(t, carry):
            r0 = pl.multiple_of(t * KEY_TILE, KEY_TILE)
            m = jnp.where(sel(keys_ref[pl.ds(r0, KEY_TILE), :], r0), 1.0, 0.0)
            mask_ref[pl.ds(r0, KEY_TILE), :] = m.astype(BF16)
            return carry
        lax.fori_loop(0, n_kt, body, 0)

    write_mask(lambda c, r0: c >= thr_c)

    excess = jnp.where((n_ge > kf) & (lane_q < n_valid), 1.0, 0.0)

    @pl.when(jnp.max(excess) > 0.0)
    def _():
        need = kf - count(lambda c, r0: c > thr)
        idx_iota = lax.broadcasted_iota(I32, (COUNT_ROWS, tq), 0)
        last = jnp.zeros((1, tq), I32)
        for bit in range(13, -1, -1):
            cand = last | (1 << bit)
            cnt = count(lambda c, r0: (c == thr) & (r0 + idx_iota < cand))
            last = jnp.where(cnt < need, cand, last)
        tile_iota = lax.broadcasted_iota(I32, (KEY_TILE, tq), 0)
        write_mask(lambda c, r0: (c > thr) | ((c == thr) & (c > INT_MIN) & (r0 + tile_iota <= last)))

    eye = jnp.where(lax.broadcasted_iota(I32, (tq, tq), 0) == lax.broadcasted_iota(I32, (tq, tq), 1),
                    1.0, 0.0).astype(BF16)

    def bias_tile(t, carry):
        r0 = pl.multiple_of(t * KEY_TILE, KEY_TILE)
        sel = _nt_dot(eye, mask_ref[pl.ds(r0, KEY_TILE), :])
        bias_ref[t] = (sel - 1.0) * (-MASK_BIAS)
        return carry

    lax.fori_loop(0, n_kt, bias_tile, 0)

    c2 = (HEAD_DIM ** -0.5) * 1.4426950408889634
    hpg = N_HEADS // N_KV_HEADS
    for g in range(N_KV_HEADS):
        qs_ref[g] = jnp.concatenate(
            [q_ref[0, :, (g * hpg + hh) * HEAD_DIM:(g * hpg + hh + 1) * HEAD_DIM] for hh in range(hpg)], axis=0)
    m_ref[...] = jnp.full(m_ref.shape, MASK_BIAS, F32)
    acc_ref[...] = jnp.zeros(acc_ref.shape, F32)
    ones = jnp.ones((KEY_TILE, HEAD_DIM), BF16)

    def attn_step(t, carry):
        r0 = pl.multiple_of(t * KEY_TILE, KEY_TILE)
        for g in range(N_KV_HEADS):
            gcols = slice(g * HEAD_DIM, (g + 1) * HEAD_DIM)
            kt = jnp.concatenate([kt_ref[2 * t + u, gcols, :] for u in range(KEY_TILE // LANES)], axis=1)
            s_ref[...] = jnp.dot(qs_ref[g], kt, preferred_element_type=F32)
            for rb in range(hpg * nqt):
                rows = slice(rb * Q_ROWS, (rb + 1) * Q_ROWS)
                qrows = slice((rb % nqt) * Q_ROWS, (rb % nqt + 1) * Q_ROWS)
                s = s_ref[rows, :] + bias_ref[t, qrows, :]
                s0, s1 = s[:, :LANES], s[:, LANES:]
                m_old = m_ref[g, rows, :]
                m_new = jnp.maximum(m_old, jnp.max(jnp.maximum(s0, s1), axis=-1, keepdims=True))
                alpha_ref[rows, :] = jnp.exp2((m_old - m_new) * c2)
                m_ref[g, rows, :] = m_new
                p_ref[rows, :LANES] = jnp.exp2((s0 - m_new) * c2).astype(BF16)
                p_ref[rows, LANES:] = jnp.exp2((s1 - m_new) * c2).astype(BF16)
            vt = jnp.concatenate([v_ref[0, pl.ds(r0, KEY_TILE), gcols], ones], axis=1)
            a = alpha_ref[...]
            acc_ref[g] = acc_ref[g] * jnp.concatenate([a, a], axis=1) + jnp.dot(
                p_ref[...], vt, preferred_element_type=F32)
        return carry

    lax.fori_loop(0, n_kt, attn_step, 0)
    for h in range(N_HEADS):
        g, hh = divmod(h, hpg)
        rows = slice(hh * tq, (hh + 1) * tq)
        o_ref[0, :, h * HEAD_DIM:(h + 1) * HEAD_DIM] = (
            acc_ref[g, rows, :HEAD_DIM] / acc_ref[g, rows, HEAD_DIM:]).astype(BF16)


def _sparse_attn(q, qit, wt, ki, kt, v, *, tq, pos0, l_true, n_valid):
    b, t_q, _ = q.shape
    lp = v.shape[1]
    nq = t_q // tq
    nqt = tq // LANES
    k_sel = min(TOPK_MAX, l_true // 4)
    hpg = N_HEADS // N_KV_HEADS
    body = functools.partial(_sparse_attn_body, tq=tq, pos0=pos0, l_true=l_true,
                             n_valid=n_valid, k_sel=k_sel)
    qspec = pl.BlockSpec((1, tq, O_Q), lambda i, j: (i, j, 0))
    kspec = pl.BlockSpec((1, lp, 2 * LANES), lambda i, j: (i, 0, 0))
    return pl.pallas_call(
        body,
        out_shape=jax.ShapeDtypeStruct((b, t_q, O_Q), BF16),
        grid=(b, nq),
        in_specs=[qspec,
                  pl.BlockSpec((nqt, N_IDX_HEADS * LANES, LANES), lambda i, j: (i * nq + j, 0, 0)),
                  pl.BlockSpec((N_IDX_HEADS, tq), lambda i, j: (0, i * nq + j)),
                  kspec,
                  pl.BlockSpec((lp // LANES, 2 * LANES, LANES), lambda i, j: (i, 0, 0)),
                  kspec],
        out_specs=qspec,
        scratch_shapes=[pltpu.VMEM((lp, tq), I32), pltpu.VMEM((lp, tq), BF16),
                        pltpu.VMEM((lp // KEY_TILE, tq, KEY_TILE), F32),
                        pltpu.VMEM((KEY_TILE, tq), F32),
                        pltpu.VMEM((N_KV_HEADS, hpg * tq, HEAD_DIM), BF16),
                        pltpu.VMEM((hpg * tq, KEY_TILE), F32),
                        pltpu.VMEM((hpg * tq, KEY_TILE), BF16),
                        pltpu.VMEM((hpg * tq, LANES), F32),
                        pltpu.VMEM((N_KV_HEADS, hpg * tq, LANES), F32),
                        pltpu.VMEM((N_KV_HEADS, hpg * tq, 2 * HEAD_DIM), F32)],
        compiler_params=_cparams(("parallel", "arbitrary")),
        name="sparse_attn",
    )(q, qit, wt, ki, kt, v)


def _out_proj_body(x_ref, o_ref, w_ref, y_ref):
    y_ref[...] = x_ref[...] + jnp.dot(o_ref[...], w_ref[...], preferred_element_type=F32)


def _out_proj(x, o, w, *, tm):
    n = x.shape[0]
    row = lambda i: (i, 0)
    return pl.pallas_call(
        _out_proj_body,
        out_shape=jax.ShapeDtypeStruct((n, D_MODEL), F32),
        grid=(n // tm,),
        in_specs=[pl.BlockSpec((tm, D_MODEL), row), pl.BlockSpec((tm, O_Q), row),
                  pl.BlockSpec((O_Q, D_MODEL), lambda i: (0, 0))],
        out_specs=pl.BlockSpec((tm, D_MODEL), row),
        compiler_params=_cparams(("parallel",)),
        name="out_proj",
    )(x, o, w)


def _gmlp_body(x_ref, gm_ref, w_in_ref, vg_ref, ws_ref, bs_ref, w_out_ref, *rest, tm, tc, emit_v):
    if emit_v:
        y_ref, v_ref, z_ref, s_ref = rest
    else:
        y_ref, z_ref, s_ref = rest
    x = x_ref[...]
    h = _rms(x) * gm_ref[...]
    z_ref[...] = jax.nn.gelu(jnp.dot(h.astype(BF16), w_in_ref[...], preferred_element_type=F32))
    v = _rms(z_ref[:, D_MODEL:]) * vg_ref[...]
    if emit_v:
        v_ref[...] = v
    z_ref[:, D_MODEL:] = v
    gw = D_MODEL // GM_GROUPS
    r_i = lax.broadcasted_iota(I32, (GM_CHUNK, GM_CHUNK), 0)
    c_i = lax.broadcasted_iota(I32, (GM_CHUNK, GM_CHUNK), 1)
    keep = (r_i // tc == c_i // tc) & (c_i <= r_i)
    for g in range(GM_GROUPS):
        ws = jnp.where(keep, ws_ref[g], 0.0).astype(BF16)
        bias = bs_ref[g]
        cols = slice(g * gw, (g + 1) * gw)
        for ch in range(tm // GM_CHUNK):
            rows = slice(ch * GM_CHUNK, (ch + 1) * GM_CHUNK)
            vb = z_ref[rows, D_MODEL + g * gw:D_MODEL + (g + 1) * gw].astype(BF16)
            mixed = jnp.dot(ws, vb, preferred_element_type=F32) + bias
            s_ref[rows, cols] = (z_ref[rows, cols] * mixed).astype(BF16)
    y_ref[...] = x + jnp.dot(s_ref[...], w_out_ref[...], preferred_element_type=F32)


def _gmlp(x, g_mix, w_in, v_g, ws, bs, w_out, *, tm, tc, emit_v):
    n = x.shape[0]
    row = lambda i: (i, 0)
    const2 = lambda i: (0, 0)
    out_shape = [jax.ShapeDtypeStruct((n, D_MODEL), F32)]
    out_specs = [pl.BlockSpec((tm, D_MODEL), row)]
    if emit_v:
        out_shape.append(jax.ShapeDtypeStruct((n, D_MODEL), F32))
        out_specs.append(pl.BlockSpec((tm, D_MODEL), row))
    res = pl.pallas_call(
        functools.partial(_gmlp_body, tm=tm, tc=tc, emit_v=emit_v),
        out_shape=tuple(out_shape),
        grid=(n // tm,),
        in_specs=[pl.BlockSpec((tm, D_MODEL), row), pl.BlockSpec((1, D_MODEL), const2),
                  pl.BlockSpec((D_MODEL, 2 * D_MODEL), const2), pl.BlockSpec((1, D_MODEL), const2),
                  pl.BlockSpec((GM_GROUPS, GM_CHUNK, GM_CHUNK), lambda i: (0, 0, 0)),
                  pl.BlockSpec((GM_GROUPS, GM_CHUNK, GM_CHUNK), lambda i: (0, 0, 0)),
                  pl.BlockSpec((D_MODEL, D_MODEL), const2)],
        out_specs=tuple(out_specs),
        scratch_shapes=[pltpu.VMEM((tm, 2 * D_MODEL), F32), pltpu.VMEM((tm, D_MODEL), BF16)],
        compiler_params=_cparams(("parallel",)),
        name="gmlp",
    )(x, g_mix, w_in, v_g, ws, bs, w_out)
    return res if emit_v else (res[0], None)


HALO = 8


def _ffn_body(*refs, tm, nt, fc, has_past):
    if has_past:
        (x_ref, gn_ref, wg_ref, wu_ref, cwg_ref, cwu_ref, cbg_ref, cbu_ref, wd_ref, pg_ref, pu_ref,
         y_ref, sg_ref, su_ref, hb_ref, acc_ref, ag_ref, au_ref, carg_ref, caru_ref) = refs
    else:
        (x_ref, gn_ref, wg_ref, wu_ref, cwg_ref, cwu_ref, cbg_ref, cbu_ref, wd_ref,
         y_ref, sg_ref, su_ref, hb_ref, acc_ref, ag_ref, au_ref, carg_ref, caru_ref) = refs
        pg_ref = pu_ref = None
    i = pl.program_id(0)
    c = pl.program_id(1)
    t = lax.rem(i, nt)

    @pl.when(c == 0)
    def _():
        hb_ref[...] = (_rms(x_ref[...]) * gn_ref[...]).astype(BF16)
        acc_ref[...] = jnp.zeros_like(acc_ref)

    def branch(w_ref, cw_ref, cb_ref, a_ref, car_ref, past_ref, s_ref):
        a_ref[HALO:HALO + tm, :] = jnp.dot(hb_ref[...], w_ref[...], preferred_element_type=F32)

        @pl.when(t == 0)
        def _():
            if past_ref is None:
                a_ref[HALO - 2:HALO, :] = jnp.zeros((CONV_W - 1, fc), F32)
            else:
                a_ref[HALO - 2:HALO, :] = past_ref[0]

        @pl.when(t > 0)
        def _():
            a_ref[HALO - 2:HALO, :] = car_ref[c, 0:CONV_W - 1, :]

        tail = a_ref[HALO + tm - 2:HALO + tm, :]
        car_ref[c, 0:CONV_W - 1, :] = tail
        s_ref[0] = tail
        cw = cw_ref[...]
        return (cb_ref[...] + cw[2:3, :] * a_ref[HALO:HALO + tm, :]
                + cw[1:2, :] * a_ref[HALO - 1:HALO - 1 + tm, :]
                + cw[0:1, :] * a_ref[HALO - 2:HALO - 2 + tm, :])

    gate = branch(wg_ref, cwg_ref, cbg_ref, ag_ref, carg_ref, pg_ref, sg_ref)
    up = branch(wu_ref, cwu_ref, cbu_ref, au_ref, caru_ref, pu_ref, su_ref)
    s = (gate * jax.nn.sigmoid(gate) * up).astype(BF16)
    acc_ref[...] += jnp.dot(s, wd_ref[...], preferred_element_type=F32)

    @pl.when(c == pl.num_programs(1) - 1)
    def _():
        y_ref[...] = x_ref[...] + acc_ref[...]


def _conv_ffn(x, g_norm, w_up, conv_w, conv_b, w_down, past, *, nb, tm, fc):
    n = x.shape[0]
    nt = n // nb // tm
    nc = D_FF // fc
    has_past = past is not None
    row = lambda i, c: (i, 0)
    in_specs = [
        pl.BlockSpec((tm, D_MODEL), row),
        pl.BlockSpec((1, D_MODEL), lambda i, c: (0, 0)),
        pl.BlockSpec((D_MODEL, fc), lambda i, c: (0, c)),
        pl.BlockSpec((D_MODEL, fc), lambda i, c: (0, nc + c)),
        pl.BlockSpec((CONV_W, fc), lambda i, c: (0, c)),
        pl.BlockSpec((CONV_W, fc), lambda i, c: (0, nc + c)),
        pl.BlockSpec((1, fc), lambda i, c: (0, c)),
        pl.BlockSpec((1, fc), lambda i, c: (0, nc + c)),
        pl.BlockSpec((fc, D_MODEL), lambda i, c: (c, 0)),
    ]
    args = [x, g_norm, w_up, w_up, conv_w, conv_w, conv_b, conv_b, w_down]
    if has_past:
        in_specs += [pl.BlockSpec((1, CONV_W - 1, fc), lambda i, c: (i // nt, 0, c)),
                     pl.BlockSpec((1, CONV_W - 1, fc), lambda i, c: (i // nt, 0, nc + c))]
        args += [past, past]
    state = jax.ShapeDtypeStruct((n // tm, CONV_W - 1, D_FF), F32)
    y, sg, su = pl.pallas_call(
        functools.partial(_ffn_body, tm=tm, nt=nt, fc=fc, has_past=has_past),
        out_shape=(jax.ShapeDtypeStruct((n, D_MODEL), F32), state, state),
        grid=(n // tm, nc),
        in_specs=in_specs,
        out_specs=(pl.BlockSpec((tm, D_MODEL), row),
                   pl.BlockSpec((1, CONV_W - 1, fc), lambda i, c: (i, 0, c)),
                   pl.BlockSpec((1, CONV_W - 1, fc), lambda i, c: (i, 0, c))),
        scratch_shapes=[pltpu.VMEM((tm, D_MODEL), BF16), pltpu.VMEM((tm, D_MODEL), F32),
                        pltpu.VMEM((tm + HALO, fc), F32), pltpu.VMEM((tm + HALO, fc), F32),
                        pltpu.VMEM((nc, HALO, fc), F32), pltpu.VMEM((nc, HALO, fc), F32)],
        compiler_params=_cparams(("arbitrary", "arbitrary")),
        name="conv_ffn",
    )(*args)
    tails = jnp.concatenate([sg, su], axis=-1).reshape(nb, nt, CONV_W - 1, 2 * D_FF)
    return y, tails[:, nt - 1]


def _ple_body(x_ref, p_ref, g_ref, wg_ref, wp_ref, y_ref):
    x = x_ref[...]
    h = (_rms(x) * g_ref[...]).astype(BF16)
    gate = jax.nn.sigmoid(jnp.dot(h, wg_ref[...], preferred_element_type=F32))
    proj = jnp.dot(p_ref[...].astype(BF16), wp_ref[...], preferred_element_type=F32)
    y_ref[...] = x + gate * proj


def _ple(x, p, g, w_gate, w_proj, *, tm):
    n = x.shape[0]
    pd = p.shape[1]
    row = lambda i: (i, 0)
    const = lambda i: (0, 0)
    return pl.pallas_call(
        _ple_body,
        out_shape=jax.ShapeDtypeStruct((n, D_MODEL), F32),
        grid=(n // tm,),
        in_specs=[pl.BlockSpec((tm, D_MODEL), row), pl.BlockSpec((tm, pd), row),
                  pl.BlockSpec((1, D_MODEL), const), pl.BlockSpec((D_MODEL, D_MODEL), const),
                  pl.BlockSpec((pd, D_MODEL), const)],
        out_specs=pl.BlockSpec((tm, D_MODEL), row),
        compiler_params=_cparams(("parallel",)),
        name="ple",
    )(x, p, g, w_gate, w_proj)


def _split_hi_lo_keys(ki):
    hi = ki.astype(BF16)
    lo = (ki - hi.astype(F32)).astype(BF16)
    return jnp.concatenate([hi, hi, lo, lo], axis=-1)


def _pad_rows(a, rows):
    return jnp.pad(a, ((0, 0), (0, rows - a.shape[1]), (0, 0)))


def _attn_layer(x, g_mix, w_in, q_g, k_g, w_out, cache, *, tm, tq):
    b, t, _ = x.shape
    n = b * t
    past_len = 0 if cache is None else cache[0].shape[1]
    pos = past_len + jnp.arange(t, dtype=I32)
    reps = tm // t if tm > t else 1
    tabs = tuple(jnp.tile(a, (reps, 1)) for a in _rope_tables(pos, HEAD_DIM) + _rope_tables(pos, IDX_DIM))
    n_tab = max(t // tm, 1)
    w_pad = jnp.pad(w_in, ((0, 0), (0, ATTN_PROJ_PAD - ATTN_PROJ))).astype(BF16)
    q, kf, vf, kif, kt, vb, kic, qit, wt = _attn_proj(
        x.reshape(n, D_MODEL), g_mix[None, :], w_pad, q_g[None, :], k_g[None, :], tabs, tm=tm, n_tab=n_tab)
    kvw = N_KV_HEADS * HEAD_DIM
    if cache is None:
        o = _sparse_attn(q.reshape(b, t, O_Q), qit, wt, kic.reshape(b, t, kvw), kt, vb.reshape(b, t, kvw),
                         tq=tq, pos0=0, l_true=t, n_valid=tq)
    else:
        ck, cv, cki = cache
        l_true = past_len + t
        lp = -(-l_true // KEY_TILE) * KEY_TILE
        k_new = kf.reshape(b, t, kvw).astype(BF16)
        k_all = _pad_rows(jnp.concatenate([ck.reshape(b, past_len, kvw).astype(BF16), k_new], 1), lp)
        kt_all = k_all.reshape(b * lp // LANES, LANES, kvw).transpose(0, 2, 1)
        v_all = _pad_rows(jnp.concatenate([cv.reshape(b, past_len, kvw).astype(BF16), vb.reshape(b, t, kvw)], 1), lp)
        ki_all = _pad_rows(jnp.concatenate([_split_hi_lo_keys(cki), kic.reshape(b, t, kvw)], 1), lp)
        q_p = _pad_rows(q.reshape(b, t, O_Q), tq)
        qit_p = jnp.pad(qit.reshape(O_Q, b, t), ((0, 0), (0, 0), (0, tq - t))).transpose(1, 0, 2)
        wt_p = jnp.pad(wt.reshape(N_IDX_HEADS, b, t), ((0, 0), (0, 0), (0, tq - t))).reshape(N_IDX_HEADS, b * tq)
        o = _sparse_attn(q_p, qit_p, wt_p, ki_all, kt_all, v_all,
                         tq=tq, pos0=past_len, l_true=l_true, n_valid=t)[:, :t]
    y = _out_proj(x.reshape(n, D_MODEL), o.reshape(n, O_Q), w_out.astype(BF16), tm=tm)
    return (y.reshape(b, t, D_MODEL), kf.reshape(b, t, N_KV_HEADS, HEAD_DIM),
            vf.reshape(b, t, N_KV_HEADS, HEAD_DIM), kif.reshape(b, t, IDX_DIM))


def _gmlp_layer(x, g_mix, w_in, v_g, w_s, b_s, w_out, *, tm, emit_v):
    b, t, _ = x.shape
    n = b * t
    tc = min(t, GM_CHUNK)
    reps = GM_CHUNK // tc
    ws = jnp.tile(w_s[:, :tc, :tc], (1, reps, reps))
    bs = jnp.broadcast_to(jnp.tile(b_s[:, :tc], (1, reps))[:, :, None], (GM_GROUPS, GM_CHUNK, GM_CHUNK))
    y, v = _gmlp(x.reshape(n, D_MODEL), g_mix[None, :], w_in.astype(BF16), v_g[None, :], ws, bs,
                 w_out.astype(BF16), tm=tm, tc=tc, emit_v=emit_v)
    return y.reshape(b, t, D_MODEL), (v.reshape(b, t, D_MODEL) if emit_v else None)


def _ffn_ple_layer(x, p, past, g_ffn, w_up, conv_w, conv_b, w_down, g_ple, w_gate, w_proj, *, tm, tm_ffn, fc):
    b, t, _ = x.shape
    n = b * t
    y, state = _conv_ffn(x.reshape(n, D_MODEL), g_ffn[None, :], w_up.astype(BF16), conv_w, conv_b[None, :],
                         w_down.astype(BF16), past, nb=b, tm=tm_ffn, fc=fc)
    y = _ple(y, p.reshape(n, p.shape[-1]), g_ple[None, :], w_gate.astype(BF16), w_proj.astype(BF16), tm=tm)
    return y.reshape(b, t, D_MODEL), state


def kernel(x_prompt, x_sample, cache_k, cache_v, cache_kidx, state_ffn_conv, p_prompt, p_sample,
           norm_mix, attn_w_in, attn_q_norm, attn_k_norm, attn_w_out,
           gmlp_w_in, gmlp_v_norm, gmlp_w_spatial, gmlp_b_spatial, gmlp_w_out,
           norm_ffn, ffn_w_up, ffn_conv_w, ffn_conv_b, ffn_w_down,
           norm_ple, ple_w_gate, ple_w_proj):
    depth = norm_mix.shape[0]
    t_s = x_sample.shape[1]
    n_s = x_sample.shape[0] * t_s
    yp, ys = x_prompt, x_sample
    kp, vp, kip, ks, vs, kis, gvs, cps, css = [], [], [], [], [], [], [], [], []
    for i in range(depth):
        j = i // 2
        if i % 2 == 0:
            yp, k, v, ki = _attn_layer(yp, norm_mix[i], attn_w_in[j], attn_q_norm[j], attn_k_norm[j],
                                       attn_w_out[j], None, tm=512, tq=256)
            kp.append(k); vp.append(v); kip.append(ki)
            ys, k, v, ki = _attn_layer(ys, norm_mix[i], attn_w_in[j], attn_q_norm[j], attn_k_norm[j],
                                       attn_w_out[j], (cache_k[j], cache_v[j], cache_kidx[j]), tm=n_s, tq=128)
            ks.append(k); vs.append(v); kis.append(ki)
        else:
            yp, _ = _gmlp_layer(yp, norm_mix[i], gmlp_w_in[j], gmlp_v_norm[j], gmlp_w_spatial[j],
                                gmlp_b_spatial[j], gmlp_w_out[j], tm=512, emit_v=False)
            ys, gv = _gmlp_layer(ys, norm_mix[i], gmlp_w_in[j], gmlp_v_norm[j], gmlp_w_spatial[j],
                                 gmlp_b_spatial[j], gmlp_w_out[j], tm=n_s, emit_v=True)
            gvs.append(gv)
        ffn = (norm_ffn[i], ffn_w_up[i], ffn_conv_w[i], ffn_conv_b[i], ffn_w_down[i],
               norm_ple[i], ple_w_gate[i], ple_w_proj[i])
        yp, cp = _ffn_ple_layer(yp, p_prompt[i], None, *ffn, tm=512, tm_ffn=512, fc=1408)
        ys, cs = _ffn_ple_layer(ys, p_sample[i], state_ffn_conv[i], *ffn, tm=n_s, tm_ffn=t_s, fc=1408)
        cps.append(cp); css.append(cs)
    return (yp, ys, jnp.stack(kp, 0), jnp.stack(vp, 0), jnp.stack(kip, 0),
            jnp.stack(ks, 0), jnp.stack(vs, 0), jnp.stack(kis, 0), jnp.stack(gvs, 0),
            jnp.stack(cps, 0), jnp.stack(css, 0))
```

```python
import functools

import numpy as np
import jax
import jax.numpy as jnp
from jax import lax
from jax.experimental import pallas as pl
from jax.experimental.pallas import tpu as pltpu

F32 = jnp.float32
BF16 = jnp.bfloat16
I32 = jnp.int32

D_MODEL = 1024
N_HEADS = 8
HEAD_DIM = 128
N_KV_HEADS = 2
N_IDX_HEADS = 8
IDX_DIM = 64
CHUNK = 64
TOPK_MAX = 256
ROPE_THETA = 500000.0
ROPE_FRAC_DIV = 4
GM_GROUPS = 8
GM_CHUNK = 128
D_FF = 2816
CONV_W = 3
EPS = 1e-6

O_Q = N_HEADS * HEAD_DIM
O_K = O_Q + N_KV_HEADS * HEAD_DIM
O_V = O_K + N_KV_HEADS * HEAD_DIM
O_QI = O_V + N_IDX_HEADS * IDX_DIM
O_KI = O_QI + IDX_DIM
ATTN_PROJ = O_KI + N_IDX_HEADS

LANES = 128
ATTN_PROJ_PAD = 2176
INT_MIN = -2147483648
MASK_BIAS = -1e30
KEY_TILE = 256
COUNT_ROWS = 64
Q_ROWS = 128
VMEM_LIMIT = 56 * 1024 * 1024


def _cparams(sem):
    return pltpu.CompilerParams(dimension_semantics=sem, vmem_limit_bytes=VMEM_LIMIT)


def _nt_dot(a, b):
    return lax.dot_general(a, b, (((1,), (1,)), ((), ())), preferred_element_type=F32)


def _rms(x):
    return x * lax.rsqrt(jnp.mean(x * x, axis=-1, keepdims=True) + EPS)


def _rope_tables(pos, head_w):
    rot = head_w // ROPE_FRAC_DIV
    half = rot // 2
    inv = ROPE_THETA ** (-jnp.arange(half, dtype=F32) * (2.0 / rot))
    ang = pos.astype(F32)[:, None] * inv[None, :]
    cos, sin = jnp.cos(ang), jnp.sin(ang)
    j = np.arange(LANES) % head_w
    idx = j % half
    first = jnp.asarray(j < half)[None, :]
    second = jnp.asarray((j >= half) & (j < rot))[None, :]
    cos_t = jnp.where(first | second, cos[:, idx], 1.0)
    sin_t = jnp.where(first, -sin[:, idx], jnp.where(second, sin[:, idx], 0.0))
    return cos_t.astype(F32), sin_t.astype(F32)


T_Q, T_K, T_V, T_QI, T_KI, T_W = 0, 8, 10, 12, 16, 17
T_QP, T_KP, T_QIP, T_KIP = 18, 26, 28, 32
PROJ_TILES = 33
PROJ_COLS = PROJ_TILES * LANES


def _proj_column_sources():
    src = np.full((PROJ_TILES, LANES), ATTN_PROJ, np.int64)
    j = np.arange(LANES)

    def partner(base, width):
        half = width // ROPE_FRAC_DIV // 2
        jj = j % width
        out = np.full(LANES, ATTN_PROJ, np.int64)
        out[jj < half] = (base + j + half)[jj < half]
        sec = (jj >= half) & (jj < 2 * half)
        out[sec] = (base + j - half)[sec]
        return out

    for t in range(T_KI):
        src[t] = t * LANES + j
    src[T_KI] = O_QI + j % IDX_DIM
    src[T_W, :N_IDX_HEADS] = O_KI + j[:N_IDX_HEADS]
    for t in range(N_HEADS):
        src[T_QP + t] = partner(t * HEAD_DIM, HEAD_DIM)
    for t in range(N_KV_HEADS):
        src[T_KP + t] = partner(O_Q + t * HEAD_DIM, HEAD_DIM)
    for t in range(T_KI - T_QI):
        src[T_QIP + t] = partner(O_V + t * LANES, IDX_DIM)
    kip = partner(0, IDX_DIM)
    src[T_KIP] = np.where(kip == ATTN_PROJ, ATTN_PROJ, O_QI + kip % IDX_DIM)
    return src.reshape(-1)


def _attn_proj_body(x_ref, gm_ref, w_ref, qg_ref, qgp_ref, kg_ref, kgp_ref, cq_ref, sq_ref, ci_ref, si_ref,
                    q_ref, kf_ref, vf_ref, kif_ref, kt_ref, vb_ref, kic_ref, qit_ref, wt_ref,
                    z_ref, *, tm):
    x = x_ref[...]
    h = _rms(x) * gm_ref[...]
    z_ref[...] = jnp.dot(h.astype(BF16), w_ref[...], preferred_element_type=F32)

    rb = LANES
    w_scale = (N_IDX_HEADS ** -0.5) * (IDX_DIM ** -0.5)

    def tile(rows, t):
        return z_ref[rows, t * LANES:(t + 1) * LANES]

    def normed_rope(rows, t, tp, g_ref, gp_ref, c, s):
        z = tile(rows, t)
        rs = lax.rsqrt(jnp.mean(z * z, axis=-1, keepdims=True) + EPS)
        return (z * rs * g_ref[...]) * c + (tile(rows, tp) * rs * gp_ref[...]) * s

    def split(y):
        hi = y.astype(BF16).astype(F32)
        return hi, y - hi

    for r in range(tm // rb):
        rows = slice(r * rb, (r + 1) * rb)
        cq, sq = cq_ref[rows, :], sq_ref[rows, :]
        ci, si = ci_ref[rows, :], si_ref[rows, :]
        for hd in range(N_HEADS):
            cols = slice(hd * HEAD_DIM, (hd + 1) * HEAD_DIM)
            q_ref[rows, cols] = normed_rope(rows, T_Q + hd, T_QP + hd, qg_ref, qgp_ref, cq, sq).astype(BF16)
        for g in range(N_KV_HEADS):
            cols = slice(g * HEAD_DIM, (g + 1) * HEAD_DIM)
            y = normed_rope(rows, T_K + g, T_KP + g, kg_ref, kgp_ref, cq, sq)
            kf_ref[rows, cols] = y
            kt_ref[r, cols, :] = y.T.astype(BF16)
            v = tile(rows, T_V + g)
            vf_ref[rows, cols] = v
            vb_ref[rows, cols] = v.astype(BF16)
        for t in range(T_KI - T_QI):
            y = tile(rows, T_QI + t) * ci + tile(rows, T_QIP + t) * si
            hi, lo = split(y)
            hi_t, lo_t = hi.T.astype(BF16), lo.T.astype(BF16)
            for u in range(2):
                base = (2 * t + u) * LANES
                dims = slice(u * IDX_DIM, (u + 1) * IDX_DIM)
                qit_ref[r, base:base + IDX_DIM, :] = hi_t[dims, :]
                qit_ref[r, base + IDX_DIM:base + LANES, :] = lo_t[dims, :]
        y = tile(rows, T_KI) * ci + tile(rows, T_KIP) * si
        kif_ref[rows, :] = y[:, :IDX_DIM]
        hi, lo = split(y)
        kic_ref[rows, 0:LANES] = hi.astype(BF16)
        kic_ref[rows, LANES:2 * LANES] = lo.astype(BF16)
        wt_ref[:, rows] = tile(rows, T_W).T[:N_IDX_HEADS, :] * w_scale


def _attn_proj(x, g_mix, w_in, gains, tabs, *, tm, n_tab):
    n = x.shape[0]
    cq, sq, ci, si = tabs
    row = lambda i: (i, 0)
    tab = lambda i: (i % n_tab, 0)
    const = lambda i: (0, 0)
    out_shape = (
        jax.ShapeDtypeStruct((n, O_Q), BF16),
        jax.ShapeDtypeStruct((n, N_KV_HEADS * HEAD_DIM), F32),
        jax.ShapeDtypeStruct((n, N_KV_HEADS * HEAD_DIM), F32),
        jax.ShapeDtypeStruct((n, IDX_DIM), F32),
        jax.ShapeDtypeStruct((n // LANES, N_KV_HEADS * HEAD_DIM, LANES), BF16),
        jax.ShapeDtypeStruct((n, N_KV_HEADS * HEAD_DIM), BF16),
        jax.ShapeDtypeStruct((n, 2 * LANES), BF16),
        jax.ShapeDtypeStruct((n // LANES, N_IDX_HEADS * LANES, LANES), BF16),
        jax.ShapeDtypeStruct((N_IDX_HEADS, n), F32),
    )
    tile3 = lambda i: (i, 0, 0)
    rspec = lambda w: pl.BlockSpec((tm, w), row)
    out_specs = (rspec(O_Q), rspec(256), rspec(256), rspec(IDX_DIM),
                 pl.BlockSpec((tm // LANES, N_KV_HEADS * HEAD_DIM, LANES), tile3),
                 rspec(256), rspec(2 * LANES),
                 pl.BlockSpec((tm // LANES, N_IDX_HEADS * LANES, LANES), tile3),
                 pl.BlockSpec((N_IDX_HEADS, tm), lambda i: (0, i)))
    return pl.pallas_call(
        functools.partial(_attn_proj_body, tm=tm),
        out_shape=out_shape,
        grid=(n // tm,),
        in_specs=[
            pl.BlockSpec((tm, D_MODEL), row),
            pl.BlockSpec((1, D_MODEL), const),
            pl.BlockSpec((D_MODEL, PROJ_COLS), const),
            pl.BlockSpec((1, HEAD_DIM), const),
            pl.BlockSpec((1, HEAD_DIM), const),
            pl.BlockSpec((1, HEAD_DIM), const),
            pl.BlockSpec((1, HEAD_DIM), const),
            pl.BlockSpec((tm, LANES), tab),
            pl.BlockSpec((tm, LANES), tab),
            pl.BlockSpec((tm, LANES), tab),
            pl.BlockSpec((tm, LANES), tab),
        ],
        out_specs=out_specs,
        scratch_shapes=[pltpu.VMEM((tm, PROJ_COLS), F32)],
        compiler_params=_cparams(("parallel",)),
        name="attn_proj",
    )(x, g_mix, w_in, *gains, cq, sq, ci, si)


def _sparse_attn_body(q_ref, qit_ref, wt_ref, ki_ref, kt_ref, v_ref, o_ref,
                      keys_ref, mask_ref, bias_ref, sacc_ref, qs_ref, s_ref, p_ref, alpha_ref, m_ref, acc_ref,
                      *, tq, pos0, l_true, n_valid, k_sel):
    j = pl.program_id(1)
    q0 = pos0 + j * tq
    nqt = tq // LANES
    lane_q = lax.broadcasted_iota(I32, (1, tq), 1)
    qpos = q0 + lane_q
    limit = jnp.minimum((lax.shift_right_logical(qpos, 6) + 1) * CHUNK, l_true)
    lim_max = jnp.minimum((lax.shift_right_logical(q0 + tq - 1, 6) + 1) * CHUNK, l_true)
    n_kt = lax.shift_right_logical(lim_max + KEY_TILE - 1, 8)

    row_iota = lax.broadcasted_iota(I32, (KEY_TILE, tq), 0)

    def score_tile(t, carry):
        r0 = pl.multiple_of(t * KEY_TILE, KEY_TILE)
        kt = ki_ref[0, pl.ds(r0, KEY_TILE), :]
        for h in range(N_IDX_HEADS):
            qh = jnp.concatenate([qit_ref[u, h * LANES:(h + 1) * LANES, :] for u in range(nqt)], axis=1)
            s = jnp.dot(kt, jnp.concatenate([qh, qh], axis=0), preferred_element_type=F32)
            term = wt_ref[h:h + 1, :] * jnp.maximum(s, 0.0)
            if h == 0:
                sacc_ref[...] = term
            elif h < N_IDX_HEADS - 1:
                sacc_ref[...] += term
            else:
                bits = pltpu.bitcast(sacc_ref[...] + term, I32)
                key = jnp.where(bits < 0, INT_MIN - bits, bits)
                keys_ref[pl.ds(r0, KEY_TILE), :] = jnp.where(r0 + row_iota < limit, key, INT_MIN)
        return carry

    lax.fori_loop(0, n_kt, score_tile, 0)

    def count(pred):
        def body(t, acc):
            for u in range(KEY_TILE // COUNT_ROWS):
                r0 = pl.multiple_of(t * KEY_TILE + u * COUNT_ROWS, COUNT_ROWS)
                acc = acc + jnp.where(pred(keys_ref[pl.ds(r0, COUNT_ROWS), :], r0), 1.0, 0.0)
            return acc
        acc = lax.fori_loop(0, n_kt, body, jnp.zeros((COUNT_ROWS, tq), F32))
        return jnp.sum(acc, axis=0, keepdims=True)

    kf = float(k_sel)

    def search(p, ans):
        cand = ans + lax.shift_left(jnp.int32(1), 31 - p)
        cnt = count(lambda c, r0: c >= cand)
        return jnp.where(cnt >= kf, cand, ans)

    thr = lax.fori_loop(0, 32, search, jnp.full((1, tq), INT_MIN, I32))
    thr_c = jnp.maximum(thr, INT_MIN + 1)
    n_ge = count(lambda c, r0: c >= thr_c)

    def write_mask(sel):
        def body(t, carry):
            r0 = pl.multiple_of(t * KEY_TILE, KEY_TILE)
            m = jnp.where(sel(keys_ref[pl.ds(r0, KEY_TILE), :], r0), 1.0, 0.0)
            mask_ref[pl.ds(r0, KEY_TILE), :] = m.astype(BF16)
            return carry
        lax.fori_loop(0, n_kt, body, 0)

    write_mask(lambda c, r0: c >= thr_c)

    excess = jnp.where((n_ge > kf) & (lane_q < n_valid), 1.0, 0.0)

    @pl.when(jnp.max(excess) > 0.0)
    def _():
        need = kf - count(lambda c, r0: c > thr)
        idx_iota = lax.broadcasted_iota(I32, (COUNT_ROWS, tq), 0)
        last = jnp.zeros((1, tq), I32)
        for bit in range(13, -1, -1):
            cand = last | (1 << bit)
            cnt = count(lambda c, r0: (c == thr) & (r0 + idx_iota < cand))
            last = jnp.where(cnt < need, cand, last)
        tile_iota = lax.broadcasted_iota(I32, (KEY_TILE, tq), 0)
        write_mask(lambda c, r0: (c > thr) | ((c == thr) & (c > INT_MIN) & (r0 + tile_iota <= last)))

    eye = jnp.where(lax.broadcasted_iota(I32, (tq, tq), 0) == lax.broadcasted_iota(I32, (tq, tq), 1),
                    1.0, 0.0).astype(BF16)

    def bias_tile(t, carry):
        r0 = pl.multiple_of(t * KEY_TILE, KEY_TILE)
        sel = _nt_dot(eye, mask_ref[pl.ds(r0, KEY_TILE), :])
        bias_ref[t] = (sel - 1.0) * (-MASK_BIAS)
        return carry

    lax.fori_loop(0, n_kt, bias_tile, 0)

    c2 = (HEAD_DIM ** -0.5) * 1.4426950408889634
    hpg = N_HEADS // N_KV_HEADS
    for g in range(N_KV_HEADS):
        qs_ref[g] = jnp.concatenate(
            [q_ref[0, :, (g * hpg + hh) * HEAD_DIM:(g * hpg + hh + 1) * HEAD_DIM] for hh in range(hpg)], axis=0)
    m_ref[...] = jnp.full(m_ref.shape, MASK_BIAS, F32)
    acc_ref[...] = jnp.zeros(acc_ref.shape, F32)
    ones = jnp.ones((KEY_TILE, HEAD_DIM), BF16)

    def attn_step(t, carry):
        r0 = pl.multiple_of(t * KEY_TILE, KEY_TILE)
        for g in range(N_KV_HEADS):
            gcols = slice(g * HEAD_DIM, (g + 1) * HEAD_DIM)
            kt = jnp.concatenate([kt_ref[2 * t + u, gcols, :] for u in range(KEY_TILE // LANES)], axis=1)
            s_ref[...] = jnp.dot(qs_ref[g], kt, preferred_element_type=F32)
            for rb in range(hpg * nqt):
                rows = slice(rb * Q_ROWS, (rb + 1) * Q_ROWS)
                qrows = slice((rb % nqt) * Q_ROWS, (rb % nqt + 1) * Q_ROWS)
                s = s_ref[rows, :] + bias_ref[t, qrows, :]
                s0, s1 = s[:, :LANES], s[:, LANES:]
                m_old = m_ref[g, rows, :]
                m_new = jnp.maximum(m_old, jnp.max(jnp.maximum(s0, s1), axis=-1, keepdims=True))
                alpha_ref[rows, :] = jnp.exp2((m_old - m_new) * c2)
                m_ref[g, rows, :] = m_new
                p_ref[rows, :LANES] = jnp.exp2((s0 - m_new) * c2).astype(BF16)
                p_ref[rows, LANES:] = jnp.exp2((s1 - m_new) * c2).astype(BF16)
            vt = jnp.concatenate([v_ref[0, pl.ds(r0, KEY_TILE), gcols], ones], axis=1)
            a = alpha_ref[...]
            acc_ref[g] = acc_ref[g] * jnp.concatenate([a, a], axis=1) + jnp.dot(
                p_ref[...], vt, preferred_element_type=F32)
        return carry

    lax.fori_loop(0, n_kt, attn_step, 0)
    for h in range(N_HEADS):
        g, hh = divmod(h, hpg)
        rows = slice(hh * tq, (hh + 1) * tq)
        o_ref[0, :, h * HEAD_DIM:(h + 1) * HEAD_DIM] = (
            acc_ref[g, rows, :HEAD_DIM] / acc_ref[g, rows, HEAD_DIM:]).astype(BF16)


def _sparse_attn(q, qit, wt, ki, kt, v, *, tq, pos0, l_true, n_valid):
    b, t_q, _ = q.shape
    lp = v.shape[1]
    nq = t_q // tq
    nqt = tq // LANES
    k_sel = min(TOPK_MAX, l_true // 4)
    hpg = N_HEADS // N_KV_HEADS
    body = functools.partial(_sparse_attn_body, tq=tq, pos0=pos0, l_true=l_true,
                             n_valid=n_valid, k_sel=k_sel)
    qspec = pl.BlockSpec((1, tq, O_Q), lambda i, j: (i, j, 0))
    kspec = pl.BlockSpec((1, lp, 2 * LANES), lambda i, j: (i, 0, 0))
    return pl.pallas_call(
        body,
        out_shape=jax.ShapeDtypeStruct((b, t_q, O_Q), BF16),
        grid=(b, nq),
        in_specs=[qspec,
                  pl.BlockSpec((nqt, N_IDX_HEADS * LANES, LANES), lambda i, j: (i * nq + j, 0, 0)),
                  pl.BlockSpec((N_IDX_HEADS, tq), lambda i, j: (0, i * nq + j)),
                  kspec,
                  pl.BlockSpec((lp // LANES, 2 * LANES, LANES), lambda i, j: (i, 0, 0)),
                  kspec],
        out_specs=qspec,
        scratch_shapes=[pltpu.VMEM((lp, tq), I32), pltpu.VMEM((lp, tq), BF16),
                        pltpu.VMEM((lp // KEY_TILE, tq, KEY_TILE), F32),
                        pltpu.VMEM((KEY_TILE, tq), F32),
                        pltpu.VMEM((N_KV_HEADS, hpg * tq, HEAD_DIM), BF16),
                        pltpu.VMEM((hpg * tq, KEY_TILE), F32),
                        pltpu.VMEM((hpg * tq, KEY_TILE), BF16),
                        pltpu.VMEM((hpg * tq, LANES), F32),
                        pltpu.VMEM((N_KV_HEADS, hpg * tq, LANES), F32),
                        pltpu.VMEM((N_KV_HEADS, hpg * tq, 2 * HEAD_DIM), F32)],
        compiler_params=_cparams(("parallel", "arbitrary")),
        name="sparse_attn",
    )(q, qit, wt, ki, kt, v)


def _out_proj_body(x_ref, o_ref, w_ref, y_ref):
    y_ref[...] = x_ref[...] + jnp.dot(o_ref[...], w_ref[...], preferred_element_type=F32)


def _out_proj(x, o, w, *, tm):
    n = x.shape[0]
    row = lambda i: (i, 0)
    return pl.pallas_call(
        _out_proj_body,
        out_shape=jax.ShapeDtypeStruct((n, D_MODEL), F32),
        grid=(n // tm,),
        in_specs=[pl.BlockSpec((tm, D_MODEL), row), pl.BlockSpec((tm, O_Q), row),
                  pl.BlockSpec((O_Q, D_MODEL), lambda i: (0, 0))],
        out_specs=pl.BlockSpec((tm, D_MODEL), row),
        compiler_params=_cparams(("parallel",)),
        name="out_proj",
    )(x, o, w)


def _gmlp_body(x_ref, gm_ref, w_in_ref, vg_ref, ws_ref, bs_ref, w_out_ref, *rest, tm, tc, emit_v):
    if emit_v:
        y_ref, v_ref, z_ref, s_ref = rest
    else:
        y_ref, z_ref, s_ref = rest
    x = x_ref[...]
    h = _rms(x) * gm_ref[...]
    z_ref[...] = jax.nn.gelu(jnp.dot(h.astype(BF16), w_in_ref[...], preferred_element_type=F32))
    v = _rms(z_ref[:, D_MODEL:]) * vg_ref[...]
    if emit_v:
        v_ref[...] = v
    z_ref[:, D_MODEL:] = v
    gw = D_MODEL // GM_GROUPS
    r_i = lax.broadcasted_iota(I32, (GM_CHUNK, GM_CHUNK), 0)
    c_i = lax.broadcasted_iota(I32, (GM_CHUNK, GM_CHUNK), 1)
    keep = (r_i // tc == c_i // tc) & (c_i <= r_i)
    for g in range(GM_GROUPS):
        ws = jnp.where(keep, ws_ref[g], 0.0).astype(BF16)
        bias = bs_ref[g]
        cols = slice(g * gw, (g + 1) * gw)
        for ch in range(tm // GM_CHUNK):
            rows = slice(ch * GM_CHUNK, (ch + 1) * GM_CHUNK)
            vb = z_ref[rows, D_MODEL + g * gw:D_MODEL + (g + 1) * gw].astype(BF16)
            mixed = jnp.dot(ws, vb, preferred_element_type=F32) + bias
            s_ref[rows, cols] = (z_ref[rows, cols] * mixed).astype(BF16)
    y_ref[...] = x + jnp.dot(s_ref[...], w_out_ref[...], preferred_element_type=F32)


def _gmlp(x, g_mix, w_in, v_g, ws, bs, w_out, *, tm, tc, emit_v):
    n = x.shape[0]
    row = lambda i: (i, 0)
    const2 = lambda i: (0, 0)
    out_shape = [jax.ShapeDtypeStruct((n, D_MODEL), F32)]
    out_specs = [pl.BlockSpec((tm, D_MODEL), row)]
    if emit_v:
        out_shape.append(jax.ShapeDtypeStruct((n, D_MODEL), F32))
        out_specs.append(pl.BlockSpec((tm, D_MODEL), row))
    res = pl.pallas_call(
        functools.partial(_gmlp_body, tm=tm, tc=tc, emit_v=emit_v),
        out_shape=tuple(out_shape),
        grid=(n // tm,),
        in_specs=[pl.BlockSpec((tm, D_MODEL), row), pl.BlockSpec((1, D_MODEL), const2),
                  pl.BlockSpec((D_MODEL, 2 * D_MODEL), const2), pl.BlockSpec((1, D_MODEL), const2),
                  pl.BlockSpec((GM_GROUPS, GM_CHUNK, GM_CHUNK), lambda i: (0, 0, 0)),
                  pl.BlockSpec((GM_GROUPS, GM_CHUNK, GM_CHUNK), lambda i: (0, 0, 0)),
                  pl.BlockSpec((D_MODEL, D_MODEL), const2)],
        out_specs=tuple(out_specs),
        scratch_shapes=[pltpu.VMEM((tm, 2 * D_MODEL), F32), pltpu.VMEM((tm, D_MODEL), BF16)],
        compiler_params=_cparams(("parallel",)),
        name="gmlp",
    )(x, g_mix, w_in, v_g, ws, bs, w_out)
    return res if emit_v else (res[0], None)


HALO = 8


FFN_COLS = 256


def _ffn_body(*refs, tm, nt, has_past):
    if has_past:
        (x_ref, gn_ref, wup_ref, cw_ref, cb_ref, wd_ref, p_ref, gp_ref, wgate_ref, wproj_ref, past_ref,
         y_ref, st_ref, hb_ref, stage_ref, s_ref, car_ref) = refs
    else:
        (x_ref, gn_ref, wup_ref, cw_ref, cb_ref, wd_ref, p_ref, gp_ref, wgate_ref, wproj_ref,
         y_ref, st_ref, hb_ref, stage_ref, s_ref, car_ref) = refs
        past_ref = None
    first = lax.rem(pl.program_id(0), nt) == 0

    @pl.when(pl.program_id(0) == 0)
    def _():
        car_ref[...] = jnp.zeros(car_ref.shape, F32)

    x = x_ref[...]
    hb_ref[...] = (_rms(x) * gn_ref[...]).astype(BF16)
    rb = min(tm, LANES)
    n_stage = D_FF // FFN_COLS

    def cols_of(k, br):
        return slice(br * D_FF + k * FFN_COLS, br * D_FF + (k + 1) * FFN_COLS)

    def up_stage(k):
        for br in range(2):
            cols = cols_of(k, br)
            if past_ref is None:
                init = jnp.zeros((CONV_W - 1, FFN_COLS), F32)
            else:
                init = past_ref[0, :, cols]
            stage_ref[k % 2, br, HALO - 2:HALO, :] = jnp.where(first, init, car_ref[0:CONV_W - 1, cols])
            stage_ref[k % 2, br, HALO:HALO + tm, :] = jnp.dot(
                hb_ref[...], wup_ref[:, cols], preferred_element_type=F32)

    def conv_stage(k):
        slot = k % 2
        for br in range(2):
            tail = stage_ref[slot, br, HALO + tm - 2:HALO + tm, :]
            car_ref[0:CONV_W - 1, cols_of(k, br)] = tail
            st_ref[0, :, cols_of(k, br)] = tail
        for r in range(tm // rb):
            for cc in range(FFN_COLS // LANES):
                lanes = slice(cc * LANES, (cc + 1) * LANES)

                def conv(br):
                    c0 = br * D_FF + k * FFN_COLS + cc * LANES
                    wcol = slice(c0, c0 + LANES)
                    r0 = HALO + r * rb
                    return (cb_ref[:, wcol]
                            + cw_ref[2:3, wcol] * stage_ref[slot, br, r0:r0 + rb, lanes]
                            + cw_ref[1:2, wcol] * stage_ref[slot, br, r0 - 1:r0 - 1 + rb, lanes]
                            + cw_ref[0:1, wcol] * stage_ref[slot, br, r0 - 2:r0 - 2 + rb, lanes])

                gate = conv(0)
                s_ref[slot, r * rb:(r + 1) * rb, lanes] = (gate * jax.nn.sigmoid(gate) * conv(1)).astype(BF16)

    up_stage(0)
    for k in range(n_stage):
        if k + 1 < n_stage:
            up_stage(k + 1)
        conv_stage(k)
        part = jnp.dot(s_ref[k % 2], wd_ref[k * FFN_COLS:(k + 1) * FFN_COLS, :], preferred_element_type=F32)
        if k == 0:
            y_ref[...] = x + part
        else:
            y_ref[...] += part

    y = y_ref[...]
    h = (_rms(y) * gp_ref[...]).astype(BF16)
    gate = jax.nn.sigmoid(jnp.dot(h, wgate_ref[...], preferred_element_type=F32))
    proj = jnp.dot(p_ref[...].astype(BF16), wproj_ref[...], preferred_element_type=F32)
    y_ref[...] = y + gate * proj


def _conv_ffn_ple(x, p, g_norm, w_up, conv_w, conv_b, w_down, g_ple, w_gate, w_proj, past, *, nb, tm, layer):
    n = x.shape[0]
    pd = p.shape[1]
    nt = n // nb // tm
    has_past = past is not None
    row = lambda i: (i, 0)
    p_row = lambda i: (layer * (n // tm) + i, 0)
    const = lambda i: (0, 0)
    in_specs = [
        pl.BlockSpec((tm, D_MODEL), row),
        pl.BlockSpec((1, D_MODEL), const),
        pl.BlockSpec((D_MODEL, 2 * D_FF), const),
        pl.BlockSpec((CONV_W, 2 * D_FF), const),
        pl.BlockSpec((1, 2 * D_FF), const),
        pl.BlockSpec((D_FF, D_MODEL), const),
        pl.BlockSpec((tm, pd), p_row),
        pl.BlockSpec((1, D_MODEL), const),
        pl.BlockSpec((D_MODEL, D_MODEL), const),
        pl.BlockSpec((pd, D_MODEL), const),
    ]
    args = [x, g_norm, w_up, conv_w, conv_b, w_down, p, g_ple, w_gate, w_proj]
    if has_past:
        in_specs.append(pl.BlockSpec((1, CONV_W - 1, 2 * D_FF), lambda i: (i // nt, 0, 0)))
        args.append(past)
    y, tails = pl.pallas_call(
        functools.partial(_ffn_body, tm=tm, nt=nt, has_past=has_past),
        out_shape=(jax.ShapeDtypeStruct((n, D_MODEL), F32),
                   jax.ShapeDtypeStruct((n // tm, CONV_W - 1, 2 * D_FF), F32)),
        grid=(n // tm,),
        in_specs=in_specs,
        out_specs=(pl.BlockSpec((tm, D_MODEL), row),
                   pl.BlockSpec((1, CONV_W - 1, 2 * D_FF), lambda i: (i, 0, 0))),
        scratch_shapes=[pltpu.VMEM((tm, D_MODEL), BF16),
                        pltpu.VMEM((2, 2, tm + HALO, FFN_COLS), F32),
                        pltpu.VMEM((2, tm, FFN_COLS), BF16),
                        pltpu.VMEM((HALO, 2 * D_FF), F32)],
        compiler_params=_cparams(("arbitrary",)),
        name="conv_ffn_ple",
    )(*args)
    return y, tails.reshape(nb, nt, CONV_W - 1, 2 * D_FF)[:, nt - 1]


def _split_hi_lo_keys(ki):
    hi = ki.astype(BF16)
    lo = (ki - hi.astype(F32)).astype(BF16)
    return jnp.concatenate([hi, hi, lo, lo], axis=-1)


def _pad_rows(a, rows):
    return jnp.pad(a, ((0, 0), (0, rows - a.shape[1]), (0, 0)))


def _attn_layer(x, g_mix, w_in, q_g, k_g, w_out, cache, *, tm, tq):
    b, t, _ = x.shape
    n = b * t
    past_len = 0 if cache is None else cache[0].shape[1]
    pos = past_len + jnp.arange(t, dtype=I32)
    reps = tm // t if tm > t else 1
    tabs = tuple(jnp.tile(a, (reps, 1)) for a in _rope_tables(pos, HEAD_DIM) + _rope_tables(pos, IDX_DIM))
    n_tab = max(t // tm, 1)
    src = _proj_column_sources()
    w_wide = jnp.take(jnp.pad(w_in.astype(BF16), ((0, 0), (0, 1))), src, axis=1)
    gsrc = src[T_QP * LANES:(T_QP + 1) * LANES]
    perm = lambda g: jnp.take(jnp.pad(g, (0, ATTN_PROJ + 1 - HEAD_DIM)), gsrc)[None, :]
    gains = (q_g[None, :], perm(q_g), k_g[None, :], perm(k_g))
    q, kf, vf, kif, kt, vb, kic, qit, wt = _attn_proj(
        x.reshape(n, D_MODEL), g_mix[None, :], w_wide, gains, tabs, tm=tm, n_tab=n_tab)
    kvw = N_KV_HEADS * HEAD_DIM
    if cache is None:
        o = _sparse_attn(q.reshape(b, t, O_Q), qit, wt, kic.reshape(b, t, kvw), kt, vb.reshape(b, t, kvw),
                         tq=tq, pos0=0, l_true=t, n_valid=tq)
    else:
        ck, cv, cki = cache
        l_true = past_len + t
        lp = -(-l_true // KEY_TILE) * KEY_TILE
        k_new = kf.reshape(b, t, kvw).astype(BF16)
        k_all = _pad_rows(jnp.concatenate([ck.reshape(b, past_len, kvw).astype(BF16), k_new], 1), lp)
        kt_all = k_all.reshape(b * lp // LANES, LANES, kvw).transpose(0, 2, 1)
        v_all = _pad_rows(jnp.concatenate([cv.reshape(b, past_len, kvw).astype(BF16), vb.reshape(b, t, kvw)], 1), lp)
        ki_all = _pad_rows(jnp.concatenate([_split_hi_lo_keys(cki), kic.reshape(b, t, kvw)], 1), lp)
        q_p = _pad_rows(q.reshape(b, t, O_Q), tq)
        qit_p = jnp.pad(qit.reshape(O_Q, b, t), ((0, 0), (0, 0), (0, tq - t))).transpose(1, 0, 2)
        wt_p = jnp.pad(wt.reshape(N_IDX_HEADS, b, t), ((0, 0), (0, 0), (0, tq - t))).reshape(N_IDX_HEADS, b * tq)
        o = _sparse_attn(q_p, qit_p, wt_p, ki_all, kt_all, v_all,
                         tq=tq, pos0=past_len, l_true=l_true, n_valid=t)[:, :t]
    y = _out_proj(x.reshape(n, D_MODEL), o.reshape(n, O_Q), w_out.astype(BF16), tm=tm)
    return (y.reshape(b, t, D_MODEL), kf.reshape(b, t, N_KV_HEADS, HEAD_DIM),
            vf.reshape(b, t, N_KV_HEADS, HEAD_DIM), kif.reshape(b, t, IDX_DIM))


def _gmlp_layer(x, g_mix, w_in, v_g, w_s, b_s, w_out, *, tm, emit_v):
    b, t, _ = x.shape
    n = b * t
    tc = min(t, GM_CHUNK)
    reps = GM_CHUNK // tc
    ws = jnp.tile(w_s[:, :tc, :tc], (1, reps, reps))
    bs = jnp.broadcast_to(jnp.tile(b_s[:, :tc], (1, reps))[:, :, None], (GM_GROUPS, GM_CHUNK, GM_CHUNK))
    y, v = _gmlp(x.reshape(n, D_MODEL), g_mix[None, :], w_in.astype(BF16), v_g[None, :], ws, bs,
                 w_out.astype(BF16), tm=tm, tc=tc, emit_v=emit_v)
    return y.reshape(b, t, D_MODEL), (v.reshape(b, t, D_MODEL) if emit_v else None)


def _ffn_ple_layer(x, p_all, layer, past, g_ffn, w_up, conv_w, conv_b, w_down, g_ple, w_gate, w_proj, *, tm):
    b, t, _ = x.shape
    n = b * t
    y, state = _conv_ffn_ple(x.reshape(n, D_MODEL), p_all.reshape(-1, p_all.shape[-1]), g_ffn[None, :],
                             w_up.astype(BF16), conv_w, conv_b[None, :], w_down.astype(BF16),
                             g_ple[None, :], w_gate.astype(BF16), w_proj.astype(BF16), past,
                             nb=b, tm=tm, layer=layer)
    return y.reshape(b, t, D_MODEL), state


def kernel(x_prompt, x_sample, cache_k, cache_v, cache_kidx, state_ffn_conv, p_prompt, p_sample,
           norm_mix, attn_w_in, attn_q_norm, attn_k_norm, attn_w_out,
           gmlp_w_in, gmlp_v_norm, gmlp_w_spatial, gmlp_b_spatial, gmlp_w_out,
           norm_ffn, ffn_w_up, ffn_conv_w, ffn_conv_b, ffn_w_down,
           norm_ple, ple_w_gate, ple_w_proj):
    depth = norm_mix.shape[0]
    t_s = x_sample.shape[1]
    n_s = x_sample.shape[0] * t_s
    yp, ys = x_prompt, x_sample
    kp, vp, kip, ks, vs, kis, gvs, cps, css = [], [], [], [], [], [], [], [], []
    for i in range(depth):
        j = i // 2
        if i % 2 == 0:
            yp, k, v, ki = _attn_layer(yp, norm_mix[i], attn_w_in[j], attn_q_norm[j], attn_k_norm[j],
                                       attn_w_out[j], None, tm=512, tq=256)
            kp.append(k); vp.append(v); kip.append(ki)
            ys, k, v, ki = _attn_layer(ys, norm_mix[i], attn_w_in[j], attn_q_norm[j], attn_k_norm[j],
                                       attn_w_out[j], (cache_k[j], cache_v[j], cache_kidx[j]), tm=n_s, tq=128)
            ks.append(k); vs.append(v); kis.append(ki)
        else:
            yp, _ = _gmlp_layer(yp, norm_mix[i], gmlp_w_in[j], gmlp_v_norm[j], gmlp_w_spatial[j],
                                gmlp_b_spatial[j], gmlp_w_out[j], tm=512, emit_v=False)
            ys, gv = _gmlp_layer(ys, norm_mix[i], gmlp_w_in[j], gmlp_v_norm[j], gmlp_w_spatial[j],
                                 gmlp_b_spatial[j], gmlp_w_out[j], tm=n_s, emit_v=True)
            gvs.append(gv)
        ffn = (norm_ffn[i], ffn_w_up[i], ffn_conv_w[i], ffn_conv_b[i], ffn_w_down[i],
               norm_ple[i], ple_w_gate[i], ple_w_proj[i])
        yp, cp = _ffn_ple_layer(yp, p_prompt, i, None, *ffn, tm=512)
        ys, cs = _ffn_ple_layer(ys, p_sample, i, state_ffn_conv[i], *ffn, tm=t_s)
        cps.append(cp); css.append(cs)
    return (yp, ys, jnp.stack(kp, 0), jnp.stack(vp, 0), jnp.stack(kip, 0),
            jnp.stack(ks, 0), jnp.stack(vs, 0), jnp.stack(kis, 0), jnp.stack(gvs, 0),
            jnp.stack(cps, 0), jnp.stack(css, 0))
```

```python
import functools

import numpy as np
import jax
import jax.numpy as jnp
from jax import lax
from jax.experimental import pallas as pl
from jax.experimental.pallas import tpu as pltpu

F32 = jnp.float32
BF16 = jnp.bfloat16
I32 = jnp.int32
I16 = jnp.int16

D_MODEL = 1024
N_HEADS = 8
HEAD_DIM = 128
N_KV_HEADS = 2
N_IDX_HEADS = 8
IDX_DIM = 64
CHUNK = 64
TOPK_MAX = 256
ROPE_THETA = 500000.0
ROPE_FRAC_DIV = 4
GM_GROUPS = 8
GM_CHUNK = 128
D_FF = 2816
CONV_W = 3
EPS = 1e-6

O_Q = N_HEADS * HEAD_DIM
O_K = O_Q + N_KV_HEADS * HEAD_DIM
O_V = O_K + N_KV_HEADS * HEAD_DIM
O_QI = O_V + N_IDX_HEADS * IDX_DIM
O_KI = O_QI + IDX_DIM
ATTN_PROJ = O_KI + N_IDX_HEADS

LANES = 128
ATTN_PROJ_PAD = 2176
INT_MIN = -2147483648
I16_MIN = -32768
MASK_BIAS = -1e30
LOGIT_SCALE = (HEAD_DIM ** -0.5) * 1.4426950408889634
KEY_TILE = 256
COUNT_ROWS = 64
Q_ROWS = 128
VMEM_LIMIT = 56 * 1024 * 1024


def _cparams(sem):
    return pltpu.CompilerParams(dimension_semantics=sem, vmem_limit_bytes=VMEM_LIMIT)


def _nt_dot(a, b):
    return lax.dot_general(a, b, (((1,), (1,)), ((), ())), preferred_element_type=F32)


def _rms(x):
    return x * lax.rsqrt(jnp.mean(x * x, axis=-1, keepdims=True) + EPS)


def _rope_tables(pos, head_w):
    rot = head_w // ROPE_FRAC_DIV
    half = rot // 2
    inv = ROPE_THETA ** (-jnp.arange(half, dtype=F32) * (2.0 / rot))
    ang = pos.astype(F32)[:, None] * inv[None, :]
    cos, sin = jnp.cos(ang), jnp.sin(ang)
    j = np.arange(LANES) % head_w
    idx = j % half
    first = jnp.asarray(j < half)[None, :]
    second = jnp.asarray((j >= half) & (j < rot))[None, :]
    cos_t = jnp.where(first | second, cos[:, idx], 1.0)
    sin_t = jnp.where(first, -sin[:, idx], jnp.where(second, sin[:, idx], 0.0))
    return cos_t.astype(F32), sin_t.astype(F32)


T_Q, T_K, T_V, T_QI, T_KI, T_W = 0, 8, 10, 12, 16, 17
T_QP, T_KP, T_QIP, T_KIP = 18, 26, 28, 32
PROJ_TILES = 33
PROJ_COLS = PROJ_TILES * LANES


def _proj_column_sources():
    src = np.full((PROJ_TILES, LANES), ATTN_PROJ, np.int64)
    j = np.arange(LANES)

    def partner(base, width):
        half = width // ROPE_FRAC_DIV // 2
        jj = j % width
        out = np.full(LANES, ATTN_PROJ, np.int64)
        out[jj < half] = (base + j + half)[jj < half]
        sec = (jj >= half) & (jj < 2 * half)
        out[sec] = (base + j - half)[sec]
        return out

    for t in range(T_KI):
        src[t] = t * LANES + j
    src[T_KI] = O_QI + j % IDX_DIM
    src[T_W, :N_IDX_HEADS] = O_KI + j[:N_IDX_HEADS]
    for t in range(N_HEADS):
        src[T_QP + t] = partner(t * HEAD_DIM, HEAD_DIM)
    for t in range(N_KV_HEADS):
        src[T_KP + t] = partner(O_Q + t * HEAD_DIM, HEAD_DIM)
    for t in range(T_KI - T_QI):
        src[T_QIP + t] = partner(O_V + t * LANES, IDX_DIM)
    kip = partner(0, IDX_DIM)
    src[T_KIP] = np.where(kip == ATTN_PROJ, ATTN_PROJ, O_QI + kip % IDX_DIM)
    return src.reshape(-1)


def _attn_proj_body(x_ref, gm_ref, w_ref, qg_ref, qgp_ref, kg_ref, kgp_ref, cq_ref, sq_ref, ci_ref, si_ref,
                    q_ref, kf_ref, vf_ref, kif_ref, kt_ref, vb_ref, kic_ref, qit_ref, wt_ref,
                    z_ref, *, tm):
    x = x_ref[...]
    h = _rms(x) * gm_ref[...]
    z_ref[...] = jnp.dot(h.astype(BF16), w_ref[...], preferred_element_type=F32)

    rb = LANES
    w_scale = (N_IDX_HEADS ** -0.5) * (IDX_DIM ** -0.5)

    def tile(rows, t):
        return z_ref[rows, t * LANES:(t + 1) * LANES]

    def normed_rope(rows, t, tp, g_ref, gp_ref, c, s):
        z = tile(rows, t)
        rs = lax.rsqrt(jnp.mean(z * z, axis=-1, keepdims=True) + EPS)
        return (z * rs * g_ref[...]) * c + (tile(rows, tp) * rs * gp_ref[...]) * s

    def split(y):
        hi = y.astype(BF16).astype(F32)
        return hi, y - hi

    for r in range(tm // rb):
        rows = slice(r * rb, (r + 1) * rb)
        cq, sq = cq_ref[rows, :], sq_ref[rows, :]
        ci, si = ci_ref[rows, :], si_ref[rows, :]
        for hd in range(N_HEADS):
            cols = slice(hd * HEAD_DIM, (hd + 1) * HEAD_DIM)
            y = normed_rope(rows, T_Q + hd, T_QP + hd, qg_ref, qgp_ref, cq, sq)
            q_ref[rows, cols] = (y * LOGIT_SCALE).astype(BF16)
        for g in range(N_KV_HEADS):
            cols = slice(g * HEAD_DIM, (g + 1) * HEAD_DIM)
            y = normed_rope(rows, T_K + g, T_KP + g, kg_ref, kgp_ref, cq, sq)
            kf_ref[rows, cols] = y
            kt_ref[r, cols, :] = y.T.astype(BF16)
            v = tile(rows, T_V + g)
            vf_ref[rows, cols] = v
            vb_ref[rows, cols] = v.astype(BF16)
        for t in range(T_KI - T_QI):
            y = tile(rows, T_QI + t) * ci + tile(rows, T_QIP + t) * si
            hi, lo = split(y)
            hi_t, lo_t = hi.T.astype(BF16), lo.T.astype(BF16)
            for u in range(2):
                base = (2 * t + u) * LANES
                dims = slice(u * IDX_DIM, (u + 1) * IDX_DIM)
                qit_ref[r, base:base + IDX_DIM, :] = hi_t[dims, :]
                qit_ref[r, base + IDX_DIM:base + LANES, :] = lo_t[dims, :]
        y = tile(rows, T_KI) * ci + tile(rows, T_KIP) * si
        kif_ref[rows, :] = y[:, :IDX_DIM]
        hi, lo = split(y)
        kic_ref[rows, 0:LANES] = hi.astype(BF16)
        kic_ref[rows, LANES:2 * LANES] = lo.astype(BF16)
        wt_ref[:, rows] = tile(rows, T_W).T[:N_IDX_HEADS, :] * w_scale


def _attn_proj(x, g_mix, w_in, gains, tabs, *, tm, n_tab):
    n = x.shape[0]
    cq, sq, ci, si = tabs
    row = lambda i: (i, 0)
    tab = lambda i: (i % n_tab, 0)
    const = lambda i: (0, 0)
    out_shape = (
        jax.ShapeDtypeStruct((n, O_Q), BF16),
        jax.ShapeDtypeStruct((n, N_KV_HEADS * HEAD_DIM), F32),
        jax.ShapeDtypeStruct((n, N_KV_HEADS * HEAD_DIM), F32),
        jax.ShapeDtypeStruct((n, IDX_DIM), F32),
        jax.ShapeDtypeStruct((n // LANES, N_KV_HEADS * HEAD_DIM, LANES), BF16),
        jax.ShapeDtypeStruct((n, N_KV_HEADS * HEAD_DIM), BF16),
        jax.ShapeDtypeStruct((n, 2 * LANES), BF16),
        jax.ShapeDtypeStruct((n // LANES, N_IDX_HEADS * LANES, LANES), BF16),
        jax.ShapeDtypeStruct((N_IDX_HEADS, n), F32),
    )
    tile3 = lambda i: (i, 0, 0)
    rspec = lambda w: pl.BlockSpec((tm, w), row)
    out_specs = (rspec(O_Q), rspec(256), rspec(256), rspec(IDX_DIM),
                 pl.BlockSpec((tm // LANES, N_KV_HEADS * HEAD_DIM, LANES), tile3),
                 rspec(256), rspec(2 * LANES),
                 pl.BlockSpec((tm // LANES, N_IDX_HEADS * LANES, LANES), tile3),
                 pl.BlockSpec((N_IDX_HEADS, tm), lambda i: (0, i)))
    return pl.pallas_call(
        functools.partial(_attn_proj_body, tm=tm),
        out_shape=out_shape,
        grid=(n // tm,),
        in_specs=[
            pl.BlockSpec((tm, D_MODEL), row),
            pl.BlockSpec((1, D_MODEL), const),
            pl.BlockSpec((D_MODEL, PROJ_COLS), const),
            pl.BlockSpec((1, HEAD_DIM), const),
            pl.BlockSpec((1, HEAD_DIM), const),
            pl.BlockSpec((1, HEAD_DIM), const),
            pl.BlockSpec((1, HEAD_DIM), const),
            pl.BlockSpec((tm, LANES), tab),
            pl.BlockSpec((tm, LANES), tab),
            pl.BlockSpec((tm, LANES), tab),
            pl.BlockSpec((tm, LANES), tab),
        ],
        out_specs=out_specs,
        scratch_shapes=[pltpu.VMEM((tm, PROJ_COLS), F32)],
        compiler_params=_cparams(("parallel",)),
        name="attn_proj",
    )(x, g_mix, w_in, *gains, cq, sq, ci, si)


def _sparse_attn_body(x_ref, q_ref, qit_ref, wt_ref, ki_ref, kt_ref, v_ref, wo_ref, y_ref,
                      keys_ref, hi_ref, lo_ref, bias_ref, sacc_ref, qs_ref, s_ref, p_ref, alpha_ref, m_ref,
                      acc_ref, o_ref, *, tq, pos0, l_true, n_valid, k_sel):
    j = pl.program_id(1)
    q0 = pos0 + j * tq
    nqt = tq // LANES
    lane_q = lax.broadcasted_iota(I32, (1, tq), 1)
    qpos = q0 + lane_q
    limit = jnp.minimum((lax.shift_right_logical(qpos, 6) + 1) * CHUNK, l_true)
    lim_max = jnp.minimum((lax.shift_right_logical(q0 + tq - 1, 6) + 1) * CHUNK, l_true)
    n_kt = lax.shift_right_logical(lim_max + KEY_TILE - 1, 8)

    row_iota = lax.broadcasted_iota(I32, (KEY_TILE, tq), 0)

    def score_tile(t, carry):
        r0 = pl.multiple_of(t * KEY_TILE, KEY_TILE)
        kt = ki_ref[0, pl.ds(r0, KEY_TILE), :]
        for h in range(N_IDX_HEADS):
            qh = jnp.concatenate([qit_ref[u, h * LANES:(h + 1) * LANES, :] for u in range(nqt)], axis=1)
            s = jnp.dot(kt, jnp.concatenate([qh, qh], axis=0), preferred_element_type=F32)
            term = wt_ref[h:h + 1, :] * jnp.maximum(s, 0.0)
            if h == 0:
                sacc_ref[...] = term
            elif h < N_IDX_HEADS - 1:
                sacc_ref[...] += term
            else:
                bits = pltpu.bitcast(sacc_ref[...] + term, I32)
                key = jnp.where(bits < 0, INT_MIN - bits, bits)
                key = jnp.where(r0 + row_iota < limit, key, INT_MIN)
                keys_ref[pl.ds(r0, KEY_TILE), :] = key
                hi_ref[pl.ds(r0, KEY_TILE), :] = lax.shift_right_arithmetic(key, 16).astype(I16)
                lo_ref[pl.ds(r0, KEY_TILE), :] = ((key & 0xFFFF) + I16_MIN).astype(I16)
        return carry

    lax.fori_loop(0, n_kt, score_tile, 0)

    def count(pred):
        def body(t, acc):
            for u in range(KEY_TILE // COUNT_ROWS):
                r0 = pl.multiple_of(t * KEY_TILE + u * COUNT_ROWS, COUNT_ROWS)
                acc = acc + jnp.where(pred(keys_ref[pl.ds(r0, COUNT_ROWS), :], r0), 1.0, 0.0)
            return acc
        acc = lax.fori_loop(0, n_kt, body, jnp.zeros((COUNT_ROWS, tq), F32))
        return jnp.sum(acc, axis=0, keepdims=True)

    def count16(ref, cand):
        c16 = jnp.broadcast_to(cand, (COUNT_ROWS, tq)).astype(I16)

        def body(t, acc):
            for u in range(KEY_TILE // COUNT_ROWS):
                r0 = pl.multiple_of(t * KEY_TILE + u * COUNT_ROWS, COUNT_ROWS)
                acc = acc + jnp.where(ref[pl.ds(r0, COUNT_ROWS), :] >= c16, jnp.int16(1), jnp.int16(0))
            return acc
        acc = lax.fori_loop(0, n_kt, body, jnp.zeros((COUNT_ROWS, tq), I16))
        return jnp.sum(acc.astype(I32).astype(F32), axis=0, keepdims=True)

    def search16(ref, need):
        def step(p, ans):
            cand = ans + lax.shift_left(jnp.int32(1), 15 - p)
            return jnp.where(count16(ref, cand) >= need, cand, ans)
        return lax.fori_loop(0, 16, step, jnp.full((1, tq), I16_MIN, I32))

    kf = float(k_sel)
    hi_k = search16(hi_ref, kf)
    n_above = jnp.where(hi_k >= -I16_MIN - 1, 0.0, count16(hi_ref, hi_k + 1))
    hi_k16 = jnp.broadcast_to(hi_k, (KEY_TILE, tq)).astype(I16)

    def keep_lo(t, carry):
        r0 = pl.multiple_of(t * KEY_TILE, KEY_TILE)
        rows = pl.ds(r0, KEY_TILE)
        lo_ref[rows, :] = jnp.where(hi_ref[rows, :] == hi_k16, lo_ref[rows, :], jnp.int16(I16_MIN))
        return carry

    lax.fori_loop(0, n_kt, keep_lo, 0)
    lo_k = search16(lo_ref, kf - n_above)
    thr = hi_k * 65536 + (lo_k - I16_MIN)
    thr_c = jnp.maximum(thr, INT_MIN + 1)
    n_ge = count(lambda c, r0: c >= thr_c)

    def to_rows(v):
        b = jnp.broadcast_to(v, (LANES, tq))
        col = jnp.concatenate([b[:, u * LANES:(u + 1) * LANES].T for u in range(nqt)], axis=0)
        return jnp.concatenate([col] * (KEY_TILE // LANES), axis=1)

    def write_bias(selected):
        def body(t, carry):
            r0 = pl.multiple_of(t * KEY_TILE, KEY_TILE)
            kq = keys_ref[pl.ds(r0, KEY_TILE), :].T
            bias_ref[t] = jnp.where(selected(kq, r0), 0.0, MASK_BIAS)
            return carry
        lax.fori_loop(0, n_kt, body, 0)

    thr_rows = to_rows(thr_c)
    write_bias(lambda kq, r0: kq >= thr_rows)

    excess = jnp.where((n_ge > kf) & (lane_q < n_valid), 1.0, 0.0)

    @pl.when(jnp.max(excess) > 0.0)
    def _():
        need = kf - count(lambda c, r0: c > thr)
        idx_iota = lax.broadcasted_iota(I32, (COUNT_ROWS, tq), 0)
        last = jnp.zeros((1, tq), I32)
        for bit in range(13, -1, -1):
            cand = last | (1 << bit)
            cnt = count(lambda c, r0: (c == thr) & (r0 + idx_iota < cand))
            last = jnp.where(cnt < need, cand, last)
        key_pos = lax.broadcasted_iota(I32, (tq, KEY_TILE), 1)
        t_rows, l_rows = to_rows(thr), to_rows(last)
        write_bias(lambda kq, r0: (kq > t_rows) | ((kq == t_rows) & (kq > INT_MIN) & (r0 + key_pos <= l_rows)))

    hpg = N_HEADS // N_KV_HEADS
    for g in range(N_KV_HEADS):
        qs_ref[g] = jnp.concatenate(
            [q_ref[0, :, (g * hpg + hh) * HEAD_DIM:(g * hpg + hh + 1) * HEAD_DIM] for hh in range(hpg)], axis=0)
    m_ref[...] = jnp.full(m_ref.shape, MASK_BIAS, F32)
    acc_ref[...] = jnp.zeros(acc_ref.shape, F32)
    ones = jnp.ones((KEY_TILE, HEAD_DIM), BF16)

    def attn_step(t, carry):
        r0 = pl.multiple_of(t * KEY_TILE, KEY_TILE)
        for g in range(N_KV_HEADS):
            gcols = slice(g * HEAD_DIM, (g + 1) * HEAD_DIM)
            kt = jnp.concatenate([kt_ref[2 * t + u, gcols, :] for u in range(KEY_TILE // LANES)], axis=1)
            s_ref[...] = jnp.dot(qs_ref[g], kt, preferred_element_type=F32)
            for rb in range(hpg * nqt):
                rows = slice(rb * Q_ROWS, (rb + 1) * Q_ROWS)
                qrows = slice((rb % nqt) * Q_ROWS, (rb % nqt + 1) * Q_ROWS)
                s = s_ref[rows, :] + bias_ref[t, qrows, :]
                s0, s1 = s[:, :LANES], s[:, LANES:]
                m_old = m_ref[g, rows, :]
                m_new = jnp.maximum(m_old, jnp.max(jnp.maximum(s0, s1), axis=-1, keepdims=True))
                alpha_ref[rows, :] = jnp.exp2(m_old - m_new)
                m_ref[g, rows, :] = m_new
                p_ref[rows, :LANES] = jnp.exp2(s0 - m_new).astype(BF16)
                p_ref[rows, LANES:] = jnp.exp2(s1 - m_new).astype(BF16)
            vt = jnp.concatenate([v_ref[0, pl.ds(r0, KEY_TILE), gcols], ones], axis=1)
            a = alpha_ref[...]
            acc_ref[g] = acc_ref[g] * jnp.concatenate([a, a], axis=1) + jnp.dot(
                p_ref[...], vt, preferred_element_type=F32)
        return carry

    lax.fori_loop(0, n_kt, attn_step, 0)
    for h in range(N_HEADS):
        g, hh = divmod(h, hpg)
        rows = slice(hh * tq, (hh + 1) * tq)
        o_ref[:, h * HEAD_DIM:(h + 1) * HEAD_DIM] = (
            acc_ref[g, rows, :HEAD_DIM] / acc_ref[g, rows, HEAD_DIM:]).astype(BF16)
    y_ref[0] = x_ref[0] + jnp.dot(o_ref[...], wo_ref[...], preferred_element_type=F32)


def _sparse_attn(x, q, qit, wt, ki, kt, v, w_out, *, tq, pos0, l_true, n_valid):
    b, t_q, _ = q.shape
    lp = v.shape[1]
    nq = t_q // tq
    nqt = tq // LANES
    k_sel = min(TOPK_MAX, l_true // 4)
    hpg = N_HEADS // N_KV_HEADS
    body = functools.partial(_sparse_attn_body, tq=tq, pos0=pos0, l_true=l_true,
                             n_valid=n_valid, k_sel=k_sel)
    xspec = pl.BlockSpec((1, tq, D_MODEL), lambda i, j: (i, j, 0))
    qspec = pl.BlockSpec((1, tq, O_Q), lambda i, j: (i, j, 0))
    kspec = pl.BlockSpec((1, lp, 2 * LANES), lambda i, j: (i, 0, 0))
    return pl.pallas_call(
        body,
        out_shape=jax.ShapeDtypeStruct((b, t_q, D_MODEL), F32),
        grid=(b, nq),
        in_specs=[xspec, qspec,
                  pl.BlockSpec((nqt, N_IDX_HEADS * LANES, LANES), lambda i, j: (i * nq + j, 0, 0)),
                  pl.BlockSpec((N_IDX_HEADS, tq), lambda i, j: (0, i * nq + j)),
                  kspec,
                  pl.BlockSpec((lp // LANES, 2 * LANES, LANES), lambda i, j: (i, 0, 0)),
                  kspec,
                  pl.BlockSpec((O_Q, D_MODEL), lambda i, j: (0, 0))],
        out_specs=xspec,
        scratch_shapes=[pltpu.VMEM((lp, tq), I32), pltpu.VMEM((lp, tq), I16), pltpu.VMEM((lp, tq), I16),
                        pltpu.VMEM((lp // KEY_TILE, tq, KEY_TILE), F32),
                        pltpu.VMEM((KEY_TILE, tq), F32),
                        pltpu.VMEM((N_KV_HEADS, hpg * tq, HEAD_DIM), BF16),
                        pltpu.VMEM((hpg * tq, KEY_TILE), F32),
                        pltpu.VMEM((hpg * tq, KEY_TILE), BF16),
                        pltpu.VMEM((hpg * tq, LANES), F32),
                        pltpu.VMEM((N_KV_HEADS, hpg * tq, LANES), F32),
                        pltpu.VMEM((N_KV_HEADS, hpg * tq, 2 * HEAD_DIM), F32),
                        pltpu.VMEM((tq, O_Q), BF16)],
        compiler_params=_cparams(("parallel", "arbitrary")),
        name="sparse_attn",
    )(x, q, qit, wt, ki, kt, v, w_out)


def _gmlp_body(x_ref, gm_ref, w_in_ref, vg_ref, ws_ref, bs_ref, w_out_ref, *rest, tm, tc, emit_v):
    if emit_v:
        y_ref, v_ref, z_ref, s_ref = rest
    else:
        y_ref, z_ref, s_ref = rest
    x = x_ref[...]
    h = _rms(x) * gm_ref[...]
    z_ref[...] = jax.nn.gelu(jnp.dot(h.astype(BF16), w_in_ref[...], preferred_element_type=F32))
    v = _rms(z_ref[:, D_MODEL:]) * vg_ref[...]
    if emit_v:
        v_ref[...] = v
    z_ref[:, D_MODEL:] = v
    gw = D_MODEL // GM_GROUPS
    r_i = lax.broadcasted_iota(I32, (GM_CHUNK, GM_CHUNK), 0)
    c_i = lax.broadcasted_iota(I32, (GM_CHUNK, GM_CHUNK), 1)
    keep = (r_i // tc == c_i // tc) & (c_i <= r_i)
    for g in range(GM_GROUPS):
        ws = jnp.where(keep, ws_ref[g], 0.0).astype(BF16)
        bias = bs_ref[g]
        cols = slice(g * gw, (g + 1) * gw)
        for ch in range(tm // GM_CHUNK):
            rows = slice(ch * GM_CHUNK, (ch + 1) * GM_CHUNK)
            vb = z_ref[rows, D_MODEL + g * gw:D_MODEL + (g + 1) * gw].astype(BF16)
            mixed = jnp.dot(ws, vb, preferred_element_type=F32) + bias
            s_ref[rows, cols] = (z_ref[rows, cols] * mixed).astype(BF16)
    y_ref[...] = x + jnp.dot(s_ref[...], w_out_ref[...], preferred_element_type=F32)


def _gmlp(x, g_mix, w_in, v_g, ws, bs, w_out, *, tm, tc, emit_v):
    n = x.shape[0]
    row = lambda i: (i, 0)
    const2 = lambda i: (0, 0)
    out_shape = [jax.ShapeDtypeStruct((n, D_MODEL), F32)]
    out_specs = [pl.BlockSpec((tm, D_MODEL), row)]
    if emit_v:
        out_shape.append(jax.ShapeDtypeStruct((n, D_MODEL), F32))
        out_specs.append(pl.BlockSpec((tm, D_MODEL), row))
    res = pl.pallas_call(
        functools.partial(_gmlp_body, tm=tm, tc=tc, emit_v=emit_v),
        out_shape=tuple(out_shape),
        grid=(n // tm,),
        in_specs=[pl.BlockSpec((tm, D_MODEL), row), pl.BlockSpec((1, D_MODEL), const2),
                  pl.BlockSpec((D_MODEL, 2 * D_MODEL), const2), pl.BlockSpec((1, D_MODEL), const2),
                  pl.BlockSpec((GM_GROUPS, GM_CHUNK, GM_CHUNK), lambda i: (0, 0, 0)),
                  pl.BlockSpec((GM_GROUPS, GM_CHUNK, GM_CHUNK), lambda i: (0, 0, 0)),
                  pl.BlockSpec((D_MODEL, D_MODEL), const2)],
        out_specs=tuple(out_specs),
        scratch_shapes=[pltpu.VMEM((tm, 2 * D_MODEL), F32), pltpu.VMEM((tm, D_MODEL), BF16)],
        compiler_params=_cparams(("parallel",)),
        name="gmlp",
    )(x, g_mix, w_in, v_g, ws, bs, w_out)
    return res if emit_v else (res[0], None)


HALO = 8


FFN_COLS = 256


def _ffn_body(*refs, tm, nt, has_past):
    if has_past:
        (x_ref, gn_ref, wup_ref, cw_ref, cb_ref, wd_ref, p_ref, gp_ref, wgate_ref, wproj_ref, past_ref,
         y_ref, st_ref, hb_ref, stage_ref, s_ref, car_ref) = refs
    else:
        (x_ref, gn_ref, wup_ref, cw_ref, cb_ref, wd_ref, p_ref, gp_ref, wgate_ref, wproj_ref,
         y_ref, st_ref, hb_ref, stage_ref, s_ref, car_ref) = refs
        past_ref = None
    first = lax.rem(pl.program_id(0), nt) == 0

    @pl.when(pl.program_id(0) == 0)
    def _():
        car_ref[...] = jnp.zeros(car_ref.shape, F32)

    x = x_ref[...]
    hb_ref[...] = (_rms(x) * gn_ref[...]).astype(BF16)
    rb = min(tm, LANES)
    n_stage = D_FF // FFN_COLS

    def cols_of(k, br):
        return slice(br * D_FF + k * FFN_COLS, br * D_FF + (k + 1) * FFN_COLS)

    def up_stage(k):
        for br in range(2):
            cols = cols_of(k, br)
            if past_ref is None:
                init = jnp.zeros((CONV_W - 1, FFN_COLS), F32)
            else:
                init = past_ref[0, :, cols]
            stage_ref[k % 2, br, HALO - 2:HALO, :] = jnp.where(first, init, car_ref[0:CONV_W - 1, cols])
            stage_ref[k % 2, br, HALO:HALO + tm, :] = jnp.dot(
                hb_ref[...], wup_ref[:, cols], preferred_element_type=F32)

    def conv_stage(k):
        slot = k % 2
        for br in range(2):
            tail = stage_ref[slot, br, HALO + tm - 2:HALO + tm, :]
            car_ref[0:CONV_W - 1, cols_of(k, br)] = tail
            st_ref[0, :, cols_of(k, br)] = tail
        for r in range(tm // rb):
            for cc in range(FFN_COLS // LANES):
                lanes = slice(cc * LANES, (cc + 1) * LANES)

                def conv(br):
                    c0 = br * D_FF + k * FFN_COLS + cc * LANES
                    wcol = slice(c0, c0 + LANES)
                    r0 = HALO + r * rb
                    return (cb_ref[:, wcol]
                            + cw_ref[2:3, wcol] * stage_ref[slot, br, r0:r0 + rb, lanes]
                            + cw_ref[1:2, wcol] * stage_ref[slot, br, r0 - 1:r0 - 1 + rb, lanes]
                            + cw_ref[0:1, wcol] * stage_ref[slot, br, r0 - 2:r0 - 2 + rb, lanes])

                gate = conv(0)
                s_ref[slot, r * rb:(r + 1) * rb, lanes] = (gate * jax.nn.sigmoid(gate) * conv(1)).astype(BF16)

    up_stage(0)
    for k in range(n_stage):
        if k + 1 < n_stage:
            up_stage(k + 1)
        conv_stage(k)
        part = jnp.dot(s_ref[k % 2], wd_ref[k * FFN_COLS:(k + 1) * FFN_COLS, :], preferred_element_type=F32)
        if k == 0:
            y_ref[...] = x + part
        else:
            y_ref[...] += part

    y = y_ref[...]
    h = (_rms(y) * gp_ref[...]).astype(BF16)
    gate = jax.nn.sigmoid(jnp.dot(h, wgate_ref[...], preferred_element_type=F32))
    proj = jnp.dot(p_ref[...].astype(BF16), wproj_ref[...], preferred_element_type=F32)
    y_ref[...] = y + gate * proj


def _conv_ffn_ple(x, p, g_norm, w_up, conv_w, conv_b, w_down, g_ple, w_gate, w_proj, past, *, nb, tm, layer):
    n = x.shape[0]
    pd = p.shape[1]
    nt = n // nb // tm
    has_past = past is not None
    row = lambda i: (i, 0)
    p_row = lambda i: (layer * (n // tm) + i, 0)
    const = lambda i: (0, 0)
    in_specs = [
        pl.BlockSpec((tm, D_MODEL), row),
        pl.BlockSpec((1, D_MODEL), const),
        pl.BlockSpec((D_MODEL, 2 * D_FF), const),
        pl.BlockSpec((CONV_W, 2 * D_FF), const),
        pl.BlockSpec((1, 2 * D_FF), const),
        pl.BlockSpec((D_FF, D_MODEL), const),
        pl.BlockSpec((tm, pd), p_row),
        pl.BlockSpec((1, D_MODEL), const),
        pl.BlockSpec((D_MODEL, D_MODEL), const),
        pl.BlockSpec((pd, D_MODEL), const),
    ]
    args = [x, g_norm, w_up, conv_w, conv_b, w_down, p, g_ple, w_gate, w_proj]
    if has_past:
        in_specs.append(pl.BlockSpec((1, CONV_W - 1, 2 * D_FF), lambda i: (i // nt, 0, 0)))
        args.append(past)
    y, tails = pl.pallas_call(
        functools.partial(_ffn_body, tm=tm, nt=nt, has_past=has_past),
        out_shape=(jax.ShapeDtypeStruct((n, D_MODEL), F32),
                   jax.ShapeDtypeStruct((n // tm, CONV_W - 1, 2 * D_FF), F32)),
        grid=(n // tm,),
        in_specs=in_specs,
        out_specs=(pl.BlockSpec((tm, D_MODEL), row),
                   pl.BlockSpec((1, CONV_W - 1, 2 * D_FF), lambda i: (i, 0, 0))),
        scratch_shapes=[pltpu.VMEM((tm, D_MODEL), BF16),
                        pltpu.VMEM((2, 2, tm + HALO, FFN_COLS), F32),
                        pltpu.VMEM((2, tm, FFN_COLS), BF16),
                        pltpu.VMEM((HALO, 2 * D_FF), F32)],
        compiler_params=_cparams(("arbitrary",)),
        name="conv_ffn_ple",
    )(*args)
    return y, tails.reshape(nb, nt, CONV_W - 1, 2 * D_FF)[:, nt - 1]


def _split_hi_lo_keys(ki):
    hi = ki.astype(BF16)
    lo = (ki - hi.astype(F32)).astype(BF16)
    return jnp.concatenate([hi, hi, lo, lo], axis=-1)


def _pad_rows(a, rows):
    return jnp.pad(a, ((0, 0), (0, rows - a.shape[1]), (0, 0)))


def _attn_layer(x, g_mix, w_in, q_g, k_g, w_out, cache, *, tm, tq):
    b, t, _ = x.shape
    n = b * t
    past_len = 0 if cache is None else cache[0].shape[1]
    pos = past_len + jnp.arange(t, dtype=I32)
    reps = tm // t if tm > t else 1
    tabs = tuple(jnp.tile(a, (reps, 1)) for a in _rope_tables(pos, HEAD_DIM) + _rope_tables(pos, IDX_DIM))
    n_tab = max(t // tm, 1)
    src = _proj_column_sources()
    w_wide = jnp.take(jnp.pad(w_in.astype(BF16), ((0, 0), (0, 1))), src, axis=1)
    gsrc = src[T_QP * LANES:(T_QP + 1) * LANES]
    perm = lambda g: jnp.take(jnp.pad(g, (0, ATTN_PROJ + 1 - HEAD_DIM)), gsrc)[None, :]
    gains = (q_g[None, :], perm(q_g), k_g[None, :], perm(k_g))
    q, kf, vf, kif, kt, vb, kic, qit, wt = _attn_proj(
        x.reshape(n, D_MODEL), g_mix[None, :], w_wide, gains, tabs, tm=tm, n_tab=n_tab)
    kvw = N_KV_HEADS * HEAD_DIM
    wo = w_out.astype(BF16)
    if cache is None:
        y = _sparse_attn(x, q.reshape(b, t, O_Q), qit, wt, kic.reshape(b, t, kvw), kt, vb.reshape(b, t, kvw), wo,
                         tq=tq, pos0=0, l_true=t, n_valid=tq)
    else:
        ck, cv, cki = cache
        l_true = past_len + t
        lp = -(-l_true // KEY_TILE) * KEY_TILE
        k_new = kf.reshape(b, t, kvw).astype(BF16)
        k_all = _pad_rows(jnp.concatenate([ck.reshape(b, past_len, kvw).astype(BF16), k_new], 1), lp)
        kt_all = k_all.reshape(b * lp // LANES, LANES, kvw).transpose(0, 2, 1)
        v_all = _pad_rows(jnp.concatenate([cv.reshape(b, past_len, kvw).astype(BF16), vb.reshape(b, t, kvw)], 1), lp)
        ki_all = _pad_rows(jnp.concatenate([_split_hi_lo_keys(cki), kic.reshape(b, t, kvw)], 1), lp)
        q_p = _pad_rows(q.reshape(b, t, O_Q), tq)
        qit_p = jnp.pad(qit.reshape(O_Q, b, t), ((0, 0), (0, 0), (0, tq - t))).transpose(1, 0, 2)
        wt_p = jnp.pad(wt.reshape(N_IDX_HEADS, b, t), ((0, 0), (0, 0), (0, tq - t))).reshape(N_IDX_HEADS, b * tq)
        y = _sparse_attn(_pad_rows(x, tq), q_p, qit_p, wt_p, ki_all, kt_all, v_all, wo,
                         tq=tq, pos0=past_len, l_true=l_true, n_valid=t)[:, :t]
    return (y, kf.reshape(b, t, N_KV_HEADS, HEAD_DIM),
            vf.reshape(b, t, N_KV_HEADS, HEAD_DIM), kif.reshape(b, t, IDX_DIM))


def _gmlp_layer(x, g_mix, w_in, v_g, w_s, b_s, w_out, *, tm, emit_v):
    b, t, _ = x.shape
    n = b * t
    tc = min(t, GM_CHUNK)
    reps = GM_CHUNK // tc
    ws = jnp.tile(w_s[:, :tc, :tc], (1, reps, reps))
    bs = jnp.broadcast_to(jnp.tile(b_s[:, :tc], (1, reps))[:, :, None], (GM_GROUPS, GM_CHUNK, GM_CHUNK))
    y, v = _gmlp(x.reshape(n, D_MODEL), g_mix[None, :], w_in.astype(BF16), v_g[None, :], ws, bs,
                 w_out.astype(BF16), tm=tm, tc=tc, emit_v=emit_v)
    return y.reshape(b, t, D_MODEL), (v.reshape(b, t, D_MODEL) if emit_v else None)


def _ffn_ple_layer(x, p_all, layer, past, g_ffn, w_up, conv_w, conv_b, w_down, g_ple, w_gate, w_proj, *, tm):
    b, t, _ = x.shape
    n = b * t
    y, state = _conv_ffn_ple(x.reshape(n, D_MODEL), p_all.reshape(-1, p_all.shape[-1]), g_ffn[None, :],
                             w_up.astype(BF16), conv_w, conv_b[None, :], w_down.astype(BF16),
                             g_ple[None, :], w_gate.astype(BF16), w_proj.astype(BF16), past,
                             nb=b, tm=tm, layer=layer)
    return y.reshape(b, t, D_MODEL), state


def kernel(x_prompt, x_sample, cache_k, cache_v, cache_kidx, state_ffn_conv, p_prompt, p_sample,
           norm_mix, attn_w_in, attn_q_norm, attn_k_norm, attn_w_out,
           gmlp_w_in, gmlp_v_norm, gmlp_w_spatial, gmlp_b_spatial, gmlp_w_out,
           norm_ffn, ffn_w_up, ffn_conv_w, ffn_conv_b, ffn_w_down,
           norm_ple, ple_w_gate, ple_w_proj):
    depth = norm_mix.shape[0]
    t_s = x_sample.shape[1]
    n_s = x_sample.shape[0] * t_s
    yp, ys = x_prompt, x_sample
    kp, vp, kip, ks, vs, kis, gvs, cps, css = [], [], [], [], [], [], [], [], []
    for i in range(depth):
        j = i // 2
        if i % 2 == 0:
            yp, k, v, ki = _attn_layer(yp, norm_mix[i], attn_w_in[j], attn_q_norm[j], attn_k_norm[j],
                                       attn_w_out[j], None, tm=512, tq=256)
            kp.append(k); vp.append(v); kip.append(ki)
            ys, k, v, ki = _attn_layer(ys, norm_mix[i], attn_w_in[j], attn_q_norm[j], attn_k_norm[j],
                                       attn_w_out[j], (cache_k[j], cache_v[j], cache_kidx[j]), tm=n_s, tq=128)
            ks.append(k); vs.append(v); kis.append(ki)
        else:
            yp, _ = _gmlp_layer(yp, norm_mix[i], gmlp_w_in[j], gmlp_v_norm[j], gmlp_w_spatial[j],
                                gmlp_b_spatial[j], gmlp_w_out[j], tm=512, emit_v=False)
            ys, gv = _gmlp_layer(ys, norm_mix[i], gmlp_w_in[j], gmlp_v_norm[j], gmlp_w_spatial[j],
                                 gmlp_b_spatial[j], gmlp_w_out[j], tm=n_s, emit_v=True)
            gvs.append(gv)
        ffn = (norm_ffn[i], ffn_w_up[i], ffn_conv_w[i], ffn_conv_b[i], ffn_w_down[i],
               norm_ple[i], ple_w_gate[i], ple_w_proj[i])
        yp, cp = _ffn_ple_layer(yp, p_prompt, i, None, *ffn, tm=512)
        ys, cs = _ffn_ple_layer(ys, p_sample, i, state_ffn_conv[i], *ffn, tm=t_s)
        cps.append(cp); css.append(cs)
    return (yp, ys, jnp.stack(kp, 0), jnp.stack(vp, 0), jnp.stack(kip, 0),
            jnp.stack(ks, 0), jnp.stack(vs, 0), jnp.stack(kis, 0), jnp.stack(gvs, 0),
            jnp.stack(cps, 0), jnp.stack(css, 0))
```

```python
import functools

import numpy as np
import jax
import jax.numpy as jnp
from jax import lax
from jax.experimental import pallas as pl
from jax.experimental.pallas import tpu as pltpu

F32 = jnp.float32
BF16 = jnp.bfloat16
I32 = jnp.int32
I16 = jnp.int16

D_MODEL = 1024
N_HEADS = 8
HEAD_DIM = 128
N_KV_HEADS = 2
N_IDX_HEADS = 8
IDX_DIM = 64
CHUNK = 64
TOPK_MAX = 256
ROPE_THETA = 500000.0
ROPE_FRAC_DIV = 4
GM_GROUPS = 8
GM_CHUNK = 128
D_FF = 2816
CONV_W = 3
EPS = 1e-6

O_Q = N_HEADS * HEAD_DIM
O_K = O_Q + N_KV_HEADS * HEAD_DIM
O_V = O_K + N_KV_HEADS * HEAD_DIM
O_QI = O_V + N_IDX_HEADS * IDX_DIM
O_KI = O_QI + IDX_DIM
ATTN_PROJ = O_KI + N_IDX_HEADS

LANES = 128
ATTN_PROJ_PAD = 2176
INT_MIN = -2147483648
I16_MIN = -32768
MASK_BIAS = -1e30
LOGIT_SCALE = (HEAD_DIM ** -0.5) * 1.4426950408889634
KEY_TILE = 256
COUNT_ROWS = 64
Q_ROWS = 128
VMEM_LIMIT = 56 * 1024 * 1024


def _cparams(sem, flags=None):
    return pltpu.CompilerParams(dimension_semantics=sem, vmem_limit_bytes=VMEM_LIMIT, flags=flags)


def _nt_dot(a, b):
    return lax.dot_general(a, b, (((1,), (1,)), ((), ())), preferred_element_type=F32)


def _rms(x):
    return x * lax.rsqrt(jnp.mean(x * x, axis=-1, keepdims=True) + EPS)


def _rope_tables(pos, head_w):
    rot = head_w // ROPE_FRAC_DIV
    half = rot // 2
    inv = ROPE_THETA ** (-jnp.arange(half, dtype=F32) * (2.0 / rot))
    ang = pos.astype(F32)[:, None] * inv[None, :]
    cos, sin = jnp.cos(ang), jnp.sin(ang)
    j = np.arange(LANES) % head_w
    idx = j % half
    first = jnp.asarray(j < half)[None, :]
    second = jnp.asarray((j >= half) & (j < rot))[None, :]
    cos_t = jnp.where(first | second, cos[:, idx], 1.0)
    sin_t = jnp.where(first, -sin[:, idx], jnp.where(second, sin[:, idx], 0.0))
    return cos_t.astype(F32), sin_t.astype(F32)


T_Q, T_K, T_V, T_QI, T_KI, T_W = 0, 8, 10, 12, 16, 17
T_KP, T_QIP, T_KIP = 18, 20, 24
PROJ_TILES = 25
PROJ_COLS = PROJ_TILES * LANES
ROPE_HALF = HEAD_DIM // ROPE_FRAC_DIV // 2


def _partner_columns(base, width):
    j = np.arange(LANES)
    half = width // ROPE_FRAC_DIV // 2
    jj = j % width
    out = np.full(LANES, ATTN_PROJ, np.int64)
    out[jj < half] = (base + j + half)[jj < half]
    sec = (jj >= half) & (jj < 2 * half)
    out[sec] = (base + j - half)[sec]
    return out


def _proj_column_sources():
    src = np.full((PROJ_TILES, LANES), ATTN_PROJ, np.int64)
    j = np.arange(LANES)
    for t in range(T_KI):
        src[t] = t * LANES + j
    src[T_KI] = O_QI + j % IDX_DIM
    src[T_W, :N_IDX_HEADS] = O_KI + j[:N_IDX_HEADS]
    for t in range(N_KV_HEADS):
        src[T_KP + t] = _partner_columns(O_Q + t * HEAD_DIM, HEAD_DIM)
    for t in range(T_KI - T_QI):
        src[T_QIP + t] = _partner_columns(O_V + t * LANES, IDX_DIM)
    kip = _partner_columns(0, IDX_DIM)
    src[T_KIP] = np.where(kip == ATTN_PROJ, ATTN_PROJ, O_QI + kip % IDX_DIM)
    return src.reshape(-1)


def _attn_proj_body(x_ref, gm_ref, w_ref, qg_ref, kg_ref, kgp_ref, cq_ref, sq_ref, ci_ref, si_ref,
                    q_ref, kf_ref, vf_ref, kif_ref, kt_ref, vb_ref, kic_ref, qit_ref, wt_ref,
                    z_ref, *, tm):
    x = x_ref[...]
    h = _rms(x) * gm_ref[...]
    z_ref[...] = jnp.dot(h.astype(BF16), w_ref[...], preferred_element_type=F32)

    rb = LANES
    w_scale = (N_IDX_HEADS ** -0.5) * (IDX_DIM ** -0.5)
    first_half = lax.broadcasted_iota(I32, (rb, LANES), 1) < ROPE_HALF

    def tile(rows, t):
        return z_ref[rows, t * LANES:(t + 1) * LANES]

    def normed_rope(rows, t, tp, g_ref, gp_ref, c, s):
        z = tile(rows, t)
        rs = lax.rsqrt(jnp.mean(z * z, axis=-1, keepdims=True) + EPS)
        y = z * rs * g_ref[...]
        if tp is None:
            yp = jnp.where(first_half, pltpu.roll(y, LANES - ROPE_HALF, 1), pltpu.roll(y, ROPE_HALF, 1))
        else:
            yp = tile(rows, tp) * rs * gp_ref[...]
        return y * c + yp * s

    def split(y):
        hi = y.astype(BF16).astype(F32)
        return hi, y - hi

    for r in range(tm // rb):
        rows = slice(r * rb, (r + 1) * rb)
        cq, sq = cq_ref[rows, :], sq_ref[rows, :]
        ci, si = ci_ref[rows, :], si_ref[rows, :]
        for hd in range(N_HEADS):
            cols = slice(hd * HEAD_DIM, (hd + 1) * HEAD_DIM)
            y = normed_rope(rows, T_Q + hd, None, qg_ref, None, cq, sq)
            q_ref[rows, cols] = (y * LOGIT_SCALE).astype(BF16)
        for g in range(N_KV_HEADS):
            cols = slice(g * HEAD_DIM, (g + 1) * HEAD_DIM)
            y = normed_rope(rows, T_K + g, T_KP + g, kg_ref, kgp_ref, cq, sq)
            kf_ref[rows, g, :] = y
            kt_ref[r, cols, :] = y.T.astype(BF16)
            v = tile(rows, T_V + g)
            vf_ref[rows, g, :] = v
            vb_ref[rows, cols] = v.astype(BF16)
        for t in range(T_KI - T_QI):
            y = tile(rows, T_QI + t) * ci + tile(rows, T_QIP + t) * si
            hi, lo = split(y)
            hi_t, lo_t = hi.T.astype(BF16), lo.T.astype(BF16)
            for u in range(2):
                base = (2 * t + u) * LANES
                dims = slice(u * IDX_DIM, (u + 1) * IDX_DIM)
                qit_ref[r, base:base + IDX_DIM, :] = hi_t[dims, :]
                qit_ref[r, base + IDX_DIM:base + LANES, :] = lo_t[dims, :]
        y = tile(rows, T_KI) * ci + tile(rows, T_KIP) * si
        kif_ref[rows, :] = y[:, :IDX_DIM]
        hi, lo = split(y)
        kic_ref[rows, 0:LANES] = hi.astype(BF16)
        kic_ref[rows, LANES:2 * LANES] = lo.astype(BF16)
        wt_ref[:, rows] = tile(rows, T_W).T[:N_IDX_HEADS, :] * w_scale


def _attn_proj(x, g_mix, w_in, gains, tabs, *, tm, n_tab):
    n = x.shape[0]
    cq, sq, ci, si = tabs
    row = lambda i: (i, 0)
    tab = lambda i: (i % n_tab, 0)
    const = lambda i: (0, 0)
    out_shape = (
        jax.ShapeDtypeStruct((n, O_Q), BF16),
        jax.ShapeDtypeStruct((n, N_KV_HEADS, HEAD_DIM), F32),
        jax.ShapeDtypeStruct((n, N_KV_HEADS, HEAD_DIM), F32),
        jax.ShapeDtypeStruct((n, IDX_DIM), F32),
        jax.ShapeDtypeStruct((n // LANES, N_KV_HEADS * HEAD_DIM, LANES), BF16),
        jax.ShapeDtypeStruct((n, N_KV_HEADS * HEAD_DIM), BF16),
        jax.ShapeDtypeStruct((n, 2 * LANES), BF16),
        jax.ShapeDtypeStruct((n // LANES, N_IDX_HEADS * LANES, LANES), BF16),
        jax.ShapeDtypeStruct((N_IDX_HEADS, n), F32),
    )
    tile3 = lambda i: (i, 0, 0)
    rspec = lambda w: pl.BlockSpec((tm, w), row)
    kvspec = pl.BlockSpec((tm, N_KV_HEADS, HEAD_DIM), tile3)
    out_specs = (rspec(O_Q), kvspec, kvspec, rspec(IDX_DIM),
                 pl.BlockSpec((tm // LANES, N_KV_HEADS * HEAD_DIM, LANES), tile3),
                 rspec(256), rspec(2 * LANES),
                 pl.BlockSpec((tm // LANES, N_IDX_HEADS * LANES, LANES), tile3),
                 pl.BlockSpec((N_IDX_HEADS, tm), lambda i: (0, i)))
    return pl.pallas_call(
        functools.partial(_attn_proj_body, tm=tm),
        out_shape=out_shape,
        grid=(n // tm,),
        in_specs=[
            pl.BlockSpec((tm, D_MODEL), row),
            pl.BlockSpec((1, D_MODEL), const),
            pl.BlockSpec((D_MODEL, PROJ_COLS), const),
            pl.BlockSpec((1, HEAD_DIM), const),
            pl.BlockSpec((1, HEAD_DIM), const),
            pl.BlockSpec((1, HEAD_DIM), const),
            pl.BlockSpec((tm, LANES), tab),
            pl.BlockSpec((tm, LANES), tab),
            pl.BlockSpec((tm, LANES), tab),
            pl.BlockSpec((tm, LANES), tab),
        ],
        out_specs=out_specs,
        scratch_shapes=[pltpu.VMEM((tm, PROJ_COLS), F32)],
        compiler_params=_cparams(("parallel",)),
        name="attn_proj",
    )(x, g_mix, w_in, *gains, cq, sq, ci, si)


def _sparse_attn_body(x_ref, q_ref, qit_ref, wt_ref, ki_ref, kt_ref, v_ref, wo_ref, y_ref,
                      keys_ref, hi_ref, lo_ref, bias_ref, sacc_ref, qs_ref, s_ref, p_ref, alpha_ref, m_ref,
                      acc_ref, o_ref, *, tq, pos0, l_true, n_valid, k_sel):
    j = pl.program_id(1)
    q0 = pos0 + j * tq
    nqt = tq // LANES
    lane_q = lax.broadcasted_iota(I32, (1, tq), 1)
    qpos = q0 + lane_q
    limit = jnp.minimum((lax.shift_right_logical(qpos, 6) + 1) * CHUNK, l_true)
    lim_max = jnp.minimum((lax.shift_right_logical(q0 + tq - 1, 6) + 1) * CHUNK, l_true)
    n_kt = lax.shift_right_logical(lim_max + KEY_TILE - 1, 8)

    row_iota = lax.broadcasted_iota(I32, (KEY_TILE, tq), 0)

    def score_tile(t, carry):
        r0 = pl.multiple_of(t * KEY_TILE, KEY_TILE)
        kt = ki_ref[0, pl.ds(r0, KEY_TILE), :]
        for h in range(N_IDX_HEADS):
            qh = jnp.concatenate([qit_ref[u, h * LANES:(h + 1) * LANES, :] for u in range(nqt)], axis=1)
            s = jnp.dot(kt, jnp.concatenate([qh, qh], axis=0), preferred_element_type=F32)
            term = wt_ref[h:h + 1, :] * jnp.maximum(s, 0.0)
            if h == 0:
                sacc_ref[...] = term
            elif h < N_IDX_HEADS - 1:
                sacc_ref[...] += term
            else:
                bits = pltpu.bitcast(sacc_ref[...] + term, I32)
                key = jnp.where(bits < 0, INT_MIN - bits, bits)
                key = jnp.where(r0 + row_iota < limit, key, INT_MIN)
                keys_ref[pl.ds(r0, KEY_TILE), :] = key
                hi_ref[pl.ds(r0, KEY_TILE), :] = lax.shift_right_arithmetic(key, 16).astype(I16)
                lo_ref[pl.ds(r0, KEY_TILE), :] = ((key & 0xFFFF) + I16_MIN).astype(I16)
        return carry

    lax.fori_loop(0, n_kt, score_tile, 0)

    def count(pred):
        def body(t, acc):
            for u in range(KEY_TILE // COUNT_ROWS):
                r0 = pl.multiple_of(t * KEY_TILE + u * COUNT_ROWS, COUNT_ROWS)
                acc = acc + jnp.where(pred(keys_ref[pl.ds(r0, COUNT_ROWS), :], r0), 1.0, 0.0)
            return acc
        acc = lax.fori_loop(0, n_kt, body, jnp.zeros((COUNT_ROWS, tq), F32))
        return jnp.sum(acc, axis=0, keepdims=True)

    def count16(ref, cand):
        c16 = jnp.broadcast_to(cand, (COUNT_ROWS, tq)).astype(I16)

        def body(t, acc):
            for u in range(KEY_TILE // COUNT_ROWS):
                r0 = pl.multiple_of(t * KEY_TILE + u * COUNT_ROWS, COUNT_ROWS)
                acc = acc + jnp.where(ref[pl.ds(r0, COUNT_ROWS), :] >= c16, jnp.int16(1), jnp.int16(0))
            return acc
        acc = lax.fori_loop(0, n_kt, body, jnp.zeros((COUNT_ROWS, tq), I16))
        return jnp.sum(acc.astype(I32).astype(F32), axis=0, keepdims=True)

    def search16(ref, need):
        def step(p, ans):
            cand = ans + lax.shift_left(jnp.int32(1), 15 - p)
            return jnp.where(count16(ref, cand) >= need, cand, ans)
        return lax.fori_loop(0, 16, step, jnp.full((1, tq), I16_MIN, I32))

    kf = float(k_sel)
    hi_k = search16(hi_ref, kf)
    n_above = jnp.where(hi_k >= -I16_MIN - 1, 0.0, count16(hi_ref, hi_k + 1))
    hi_k16 = jnp.broadcast_to(hi_k, (KEY_TILE, tq)).astype(I16)

    def keep_lo(t, carry):
        r0 = pl.multiple_of(t * KEY_TILE, KEY_TILE)
        rows = pl.ds(r0, KEY_TILE)
        lo_ref[rows, :] = jnp.where(hi_ref[rows, :] == hi_k16, lo_ref[rows, :], jnp.int16(I16_MIN))
        return carry

    lax.fori_loop(0, n_kt, keep_lo, 0)
    lo_k = search16(lo_ref, kf - n_above)
    thr = hi_k * 65536 + (lo_k - I16_MIN)
    thr_c = jnp.maximum(thr, INT_MIN + 1)
    n_ge = count(lambda c, r0: c >= thr_c)

    def to_rows(v):
        b = jnp.broadcast_to(v, (LANES, tq))
        col = jnp.concatenate([b[:, u * LANES:(u + 1) * LANES].T for u in range(nqt)], axis=0)
        return jnp.concatenate([col] * (KEY_TILE // LANES), axis=1)

    def write_bias(selected):
        def body(t, carry):
            r0 = pl.multiple_of(t * KEY_TILE, KEY_TILE)
            kq = keys_ref[pl.ds(r0, KEY_TILE), :].T
            bias_ref[t] = jnp.where(selected(kq, r0), 0.0, MASK_BIAS)
            return carry
        lax.fori_loop(0, n_kt, body, 0)

    thr_rows = to_rows(thr_c)
    write_bias(lambda kq, r0: kq >= thr_rows)

    excess = jnp.where((n_ge > kf) & (lane_q < n_valid), 1.0, 0.0)

    @pl.when(jnp.max(excess) > 0.0)
    def _():
        need = kf - count(lambda c, r0: c > thr)
        idx_iota = lax.broadcasted_iota(I32, (COUNT_ROWS, tq), 0)
        last = jnp.zeros((1, tq), I32)
        for bit in range(13, -1, -1):
            cand = last | (1 << bit)
            cnt = count(lambda c, r0: (c == thr) & (r0 + idx_iota < cand))
            last = jnp.where(cnt < need, cand, last)
        key_pos = lax.broadcasted_iota(I32, (tq, KEY_TILE), 1)
        t_rows, l_rows = to_rows(thr), to_rows(last)
        write_bias(lambda kq, r0: (kq > t_rows) | ((kq == t_rows) & (kq > INT_MIN) & (r0 + key_pos <= l_rows)))

    hpg = N_HEADS // N_KV_HEADS
    for g in range(N_KV_HEADS):
        qs_ref[g] = jnp.concatenate(
            [q_ref[0, :, (g * hpg + hh) * HEAD_DIM:(g * hpg + hh + 1) * HEAD_DIM] for hh in range(hpg)], axis=0)
    m_ref[...] = jnp.full(m_ref.shape, MASK_BIAS, F32)
    acc_ref[...] = jnp.zeros(acc_ref.shape, F32)
    ones = jnp.ones((KEY_TILE, HEAD_DIM), BF16)

    def attn_step(t, carry):
        r0 = pl.multiple_of(t * KEY_TILE, KEY_TILE)
        for g in range(N_KV_HEADS):
            gcols = slice(g * HEAD_DIM, (g + 1) * HEAD_DIM)
            kt = jnp.concatenate([kt_ref[2 * t + u, gcols, :] for u in range(KEY_TILE // LANES)], axis=1)
            s_ref[...] = jnp.dot(qs_ref[g], kt, preferred_element_type=F32)
            for rb in range(hpg * nqt):
                rows = slice(rb * Q_ROWS, (rb + 1) * Q_ROWS)
                qrows = slice((rb % nqt) * Q_ROWS, (rb % nqt + 1) * Q_ROWS)
                s = s_ref[rows, :] + bias_ref[t, qrows, :]
                s0, s1 = s[:, :LANES], s[:, LANES:]
                m_old = m_ref[g, rows, :]
                m_new = jnp.maximum(m_old, jnp.max(jnp.maximum(s0, s1), axis=-1, keepdims=True))
                alpha_ref[rows, :] = jnp.exp2(m_old - m_new)
                m_ref[g, rows, :] = m_new
                p_ref[rows, :LANES] = jnp.exp2(s0 - m_new).astype(BF16)
                p_ref[rows, LANES:] = jnp.exp2(s1 - m_new).astype(BF16)
            vt = jnp.concatenate([v_ref[0, pl.ds(r0, KEY_TILE), gcols], ones], axis=1)
            a = alpha_ref[...]
            acc_ref[g] = acc_ref[g] * jnp.concatenate([a, a], axis=1) + jnp.dot(
                p_ref[...], vt, preferred_element_type=F32)
        return carry

    lax.fori_loop(0, n_kt, attn_step, 0)
    for h in range(N_HEADS):
        g, hh = divmod(h, hpg)
        rows = slice(hh * tq, (hh + 1) * tq)
        o_ref[:, h * HEAD_DIM:(h + 1) * HEAD_DIM] = (
            acc_ref[g, rows, :HEAD_DIM] / acc_ref[g, rows, HEAD_DIM:]).astype(BF16)
    y_ref[0] = x_ref[0] + jnp.dot(o_ref[...], wo_ref[...], preferred_element_type=F32)


def _sparse_attn(x, q, qit, wt, ki, kt, v, w_out, *, tq, pos0, l_true, n_valid):
    b, t_q, _ = q.shape
    lp = v.shape[1]
    nq = t_q // tq
    nqt = tq // LANES
    k_sel = min(TOPK_MAX, l_true // 4)
    hpg = N_HEADS // N_KV_HEADS
    body = functools.partial(_sparse_attn_body, tq=tq, pos0=pos0, l_true=l_true,
                             n_valid=n_valid, k_sel=k_sel)
    xspec = pl.BlockSpec((1, tq, D_MODEL), lambda i, j: (i, j, 0))
    qspec = pl.BlockSpec((1, tq, O_Q), lambda i, j: (i, j, 0))
    kspec = pl.BlockSpec((1, lp, 2 * LANES), lambda i, j: (i, 0, 0))
    return pl.pallas_call(
        body,
        out_shape=jax.ShapeDtypeStruct((b, t_q, D_MODEL), F32),
        grid=(b, nq),
        in_specs=[xspec, qspec,
                  pl.BlockSpec((nqt, N_IDX_HEADS * LANES, LANES), lambda i, j: (i * nq + j, 0, 0)),
                  pl.BlockSpec((N_IDX_HEADS, tq), lambda i, j: (0, i * nq + j)),
                  kspec,
                  pl.BlockSpec((lp // LANES, 2 * LANES, LANES), lambda i, j: (i, 0, 0)),
                  kspec,
                  pl.BlockSpec((O_Q, D_MODEL), lambda i, j: (0, 0))],
        out_specs=xspec,
        scratch_shapes=[pltpu.VMEM((lp, tq), I32), pltpu.VMEM((lp, tq), I16), pltpu.VMEM((lp, tq), I16),
                        pltpu.VMEM((lp // KEY_TILE, tq, KEY_TILE), F32),
                        pltpu.VMEM((KEY_TILE, tq), F32),
                        pltpu.VMEM((N_KV_HEADS, hpg * tq, HEAD_DIM), BF16),
                        pltpu.VMEM((hpg * tq, KEY_TILE), F32),
                        pltpu.VMEM((hpg * tq, KEY_TILE), BF16),
                        pltpu.VMEM((hpg * tq, LANES), F32),
                        pltpu.VMEM((N_KV_HEADS, hpg * tq, LANES), F32),
                        pltpu.VMEM((N_KV_HEADS, hpg * tq, 2 * HEAD_DIM), F32),
                        pltpu.VMEM((tq, O_Q), BF16)],
        compiler_params=_cparams(("parallel", "arbitrary")),
        name="sparse_attn",
    )(x, q, qit, wt, ki, kt, v, w_out)


def _gmlp_body(x_ref, gm_ref, w_in_ref, vg_ref, ws_ref, bs_ref, w_out_ref, *rest, tm, tc, emit_v):
    if emit_v:
        y_ref, v_ref, z_ref, s_ref = rest
    else:
        y_ref, z_ref, s_ref = rest
    x = x_ref[...]
    h = _rms(x) * gm_ref[...]
    z_ref[...] = jax.nn.gelu(jnp.dot(h.astype(BF16), w_in_ref[...], preferred_element_type=F32))
    v = _rms(z_ref[:, D_MODEL:]) * vg_ref[...]
    if emit_v:
        v_ref[...] = v
    z_ref[:, D_MODEL:] = v
    gw = D_MODEL // GM_GROUPS
    r_i = lax.broadcasted_iota(I32, (GM_CHUNK, GM_CHUNK), 0)
    c_i = lax.broadcasted_iota(I32, (GM_CHUNK, GM_CHUNK), 1)
    keep = (r_i // tc == c_i // tc) & (c_i <= r_i)
    for g in range(GM_GROUPS):
        ws = jnp.where(keep, ws_ref[g], 0.0).astype(BF16)
        bias = bs_ref[g]
        cols = slice(g * gw, (g + 1) * gw)
        for ch in range(tm // GM_CHUNK):
            rows = slice(ch * GM_CHUNK, (ch + 1) * GM_CHUNK)
            vb = z_ref[rows, D_MODEL + g * gw:D_MODEL + (g + 1) * gw].astype(BF16)
            mixed = jnp.dot(ws, vb, preferred_element_type=F32) + bias
            s_ref[rows, cols] = (z_ref[rows, cols] * mixed).astype(BF16)
    y_ref[...] = x + jnp.dot(s_ref[...], w_out_ref[...], preferred_element_type=F32)


def _gmlp(x, g_mix, w_in, v_g, ws, bs, w_out, *, tm, tc, emit_v):
    n = x.shape[0]
    row = lambda i: (i, 0)
    const2 = lambda i: (0, 0)
    out_shape = [jax.ShapeDtypeStruct((n, D_MODEL), F32)]
    out_specs = [pl.BlockSpec((tm, D_MODEL), row)]
    if emit_v:
        out_shape.append(jax.ShapeDtypeStruct((n, D_MODEL), F32))
        out_specs.append(pl.BlockSpec((tm, D_MODEL), row))
    res = pl.pallas_call(
        functools.partial(_gmlp_body, tm=tm, tc=tc, emit_v=emit_v),
        out_shape=tuple(out_shape),
        grid=(n // tm,),
        in_specs=[pl.BlockSpec((tm, D_MODEL), row), pl.BlockSpec((1, D_MODEL), const2),
                  pl.BlockSpec((D_MODEL, 2 * D_MODEL), const2), pl.BlockSpec((1, D_MODEL), const2),
                  pl.BlockSpec((GM_GROUPS, GM_CHUNK, GM_CHUNK), lambda i: (0, 0, 0)),
                  pl.BlockSpec((GM_GROUPS, GM_CHUNK, GM_CHUNK), lambda i: (0, 0, 0)),
                  pl.BlockSpec((D_MODEL, D_MODEL), const2)],
        out_specs=tuple(out_specs),
        scratch_shapes=[pltpu.VMEM((tm, 2 * D_MODEL), F32), pltpu.VMEM((tm, D_MODEL), BF16)],
        compiler_params=_cparams(("parallel",)),
        name="gmlp",
    )(x, g_mix, w_in, v_g, ws, bs, w_out)
    return res if emit_v else (res[0], None)


HALO = 8


FFN_COLS = 256
FFN_ROWS = 128


def _ffn_body(*refs, tm, nt, has_past):
    if has_past:
        (x_ref, gn_ref, wup_ref, cw_ref, cb_ref, wd_ref, p_ref, gp_ref, wgate_ref, wproj_ref, past_ref,
         y_ref, st_ref, hb_ref, stage_ref, s_ref, car_ref) = refs
    else:
        (x_ref, gn_ref, wup_ref, cw_ref, cb_ref, wd_ref, p_ref, gp_ref, wgate_ref, wproj_ref,
         y_ref, st_ref, hb_ref, stage_ref, s_ref, car_ref) = refs
        past_ref = None
    first = lax.rem(pl.program_id(0), nt) == 0

    @pl.when(pl.program_id(0) == 0)
    def _():
        car_ref[...] = jnp.zeros(car_ref.shape, F32)

    x = x_ref[...]
    hb_ref[...] = (_rms(x) * gn_ref[...]).astype(BF16)
    rb = min(tm, FFN_ROWS)
    n_stage = D_FF // FFN_COLS

    def cols_of(k, br):
        return slice(br * D_FF + k * FFN_COLS, br * D_FF + (k + 1) * FFN_COLS)

    def up_stage(k):
        for br in range(2):
            cols = cols_of(k, br)
            if past_ref is None:
                init = jnp.zeros((CONV_W - 1, FFN_COLS), F32)
            else:
                init = past_ref[0, :, cols]
            stage_ref[k % 2, br, HALO - 2:HALO, :] = jnp.where(first, init, car_ref[0:CONV_W - 1, cols])
            stage_ref[k % 2, br, HALO:HALO + tm, :] = jnp.dot(
                hb_ref[...], wup_ref[:, cols], preferred_element_type=F32)

    def conv_stage(k):
        slot = k % 2
        for br in range(2):
            tail = stage_ref[slot, br, HALO + tm - 2:HALO + tm, :]
            car_ref[0:CONV_W - 1, cols_of(k, br)] = tail
            st_ref[0, :, cols_of(k, br)] = tail
        for r in range(tm // rb):
            for cc in range(FFN_COLS // LANES):
                lanes = slice(cc * LANES, (cc + 1) * LANES)

                def conv(br):
                    c0 = br * D_FF + k * FFN_COLS + cc * LANES
                    wcol = slice(c0, c0 + LANES)
                    r0 = HALO + r * rb
                    return (cb_ref[:, wcol]
                            + cw_ref[2:3, wcol] * stage_ref[slot, br, r0:r0 + rb, lanes]
                            + cw_ref[1:2, wcol] * stage_ref[slot, br, r0 - 1:r0 - 1 + rb, lanes]
                            + cw_ref[0:1, wcol] * stage_ref[slot, br, r0 - 2:r0 - 2 + rb, lanes])

                gate = conv(0)
                s_ref[k % 3, r * rb:(r + 1) * rb, lanes] = (gate * jax.nn.sigmoid(gate) * conv(1)).astype(BF16)

    def down_stage(k):
        part = jnp.dot(s_ref[k % 3], wd_ref[k * FFN_COLS:(k + 1) * FFN_COLS, :], preferred_element_type=F32)
        if k == 0:
            y_ref[...] = x + part
        else:
            y_ref[...] += part

    up_stage(0)
    for k in range(n_stage):
        if k + 1 < n_stage:
            up_stage(k + 1)
        conv_stage(k)
        if k > 0:
            down_stage(k - 1)
    down_stage(n_stage - 1)

    y = y_ref[...]
    h = (_rms(y) * gp_ref[...]).astype(BF16)
    gate = jax.nn.sigmoid(jnp.dot(h, wgate_ref[...], preferred_element_type=F32))
    proj = jnp.dot(p_ref[...].astype(BF16), wproj_ref[...], preferred_element_type=F32)
    y_ref[...] = y + gate * proj


def _conv_ffn_ple(x, p, g_norm, w_up, conv_w, conv_b, w_down, g_ple, w_gate, w_proj, past, *, nb, tm, layer):
    n = x.shape[0]
    pd = p.shape[1]
    nt = n // nb // tm
    has_past = past is not None
    row = lambda i: (i, 0)
    p_row = lambda i: (layer * (n // tm) + i, 0)
    const = lambda i: (0, 0)
    in_specs = [
        pl.BlockSpec((tm, D_MODEL), row),
        pl.BlockSpec((1, D_MODEL), const),
        pl.BlockSpec((D_MODEL, 2 * D_FF), const),
        pl.BlockSpec((CONV_W, 2 * D_FF), const),
        pl.BlockSpec((1, 2 * D_FF), const),
        pl.BlockSpec((D_FF, D_MODEL), const),
        pl.BlockSpec((tm, pd), p_row),
        pl.BlockSpec((1, D_MODEL), const),
        pl.BlockSpec((D_MODEL, D_MODEL), const),
        pl.BlockSpec((pd, D_MODEL), const),
    ]
    args = [x, g_norm, w_up, conv_w, conv_b, w_down, p, g_ple, w_gate, w_proj]
    if has_past:
        in_specs.append(pl.BlockSpec((1, CONV_W - 1, 2 * D_FF), lambda i: (i // nt, 0, 0)))
        args.append(past)
    y, tails = pl.pallas_call(
        functools.partial(_ffn_body, tm=tm, nt=nt, has_past=has_past),
        out_shape=(jax.ShapeDtypeStruct((n, D_MODEL), F32),
                   jax.ShapeDtypeStruct((n // tm, CONV_W - 1, 2 * D_FF), F32)),
        grid=(n // tm,),
        in_specs=in_specs,
        out_specs=(pl.BlockSpec((tm, D_MODEL), row),
                   pl.BlockSpec((1, CONV_W - 1, 2 * D_FF), lambda i: (i, 0, 0))),
        scratch_shapes=[pltpu.VMEM((tm, D_MODEL), BF16),
                        pltpu.VMEM((2, 2, tm + HALO, FFN_COLS), F32),
                        pltpu.VMEM((3, tm, FFN_COLS), BF16),
                        pltpu.VMEM((HALO, 2 * D_FF), F32)],
        compiler_params=_cparams(("arbitrary",)),
        name="conv_ffn_ple",
    )(*args)
    return y, tails.reshape(nb, nt, CONV_W - 1, 2 * D_FF)[:, nt - 1]


def _split_hi_lo_keys(ki):
    hi = ki.astype(BF16)
    lo = (ki - hi.astype(F32)).astype(BF16)
    return jnp.concatenate([hi, hi, lo, lo], axis=-1)


def _pad_rows(a, rows):
    return jnp.pad(a, ((0, 0), (0, rows - a.shape[1]), (0, 0)))


def _attn_layer(x, g_mix, w_in, q_g, k_g, w_out, cache, *, tm, tq):
    b, t, _ = x.shape
    n = b * t
    past_len = 0 if cache is None else cache[0].shape[1]
    pos = past_len + jnp.arange(t, dtype=I32)
    reps = tm // t if tm > t else 1
    tabs = tuple(jnp.tile(a, (reps, 1)) for a in _rope_tables(pos, HEAD_DIM) + _rope_tables(pos, IDX_DIM))
    n_tab = max(t // tm, 1)
    src = _proj_column_sources()
    w_wide = jnp.take(jnp.pad(w_in.astype(BF16), ((0, 0), (0, 1))), src, axis=1)
    k_gp = jnp.take(jnp.pad(k_g, (0, ATTN_PROJ + 1 - HEAD_DIM)), _partner_columns(0, HEAD_DIM))
    gains = (q_g[None, :], k_g[None, :], k_gp[None, :])
    q, kf, vf, kif, kt, vb, kic, qit, wt = _attn_proj(
        x.reshape(n, D_MODEL), g_mix[None, :], w_wide, gains, tabs, tm=tm, n_tab=n_tab)
    kvw = N_KV_HEADS * HEAD_DIM
    wo = w_out.astype(BF16)
    if cache is None:
        y = _sparse_attn(x, q.reshape(b, t, O_Q), qit, wt, kic.reshape(b, t, kvw), kt, vb.reshape(b, t, kvw), wo,
                         tq=tq, pos0=0, l_true=t, n_valid=tq)
    else:
        ck, cv, cki = cache
        l_true = past_len + t
        lp = -(-l_true // KEY_TILE) * KEY_TILE
        k_new = kf.reshape(b, t, kvw).astype(BF16)
        k_all = _pad_rows(jnp.concatenate([ck.reshape(b, past_len, kvw).astype(BF16), k_new], 1), lp)
        kt_all = k_all.reshape(b * lp // LANES, LANES, kvw).transpose(0, 2, 1)
        v_all = _pad_rows(jnp.concatenate([cv.reshape(b, past_len, kvw).astype(BF16), vb.reshape(b, t, kvw)], 1), lp)
        ki_all = _pad_rows(jnp.concatenate([_split_hi_lo_keys(cki), kic.reshape(b, t, kvw)], 1), lp)
        q_p = _pad_rows(q.reshape(b, t, O_Q), tq)
        qit_p = jnp.pad(qit.reshape(O_Q, b, t), ((0, 0), (0, 0), (0, tq - t))).transpose(1, 0, 2)
        wt_p = jnp.pad(wt.reshape(N_IDX_HEADS, b, t), ((0, 0), (0, 0), (0, tq - t))).reshape(N_IDX_HEADS, b * tq)
        y = _sparse_attn(_pad_rows(x, tq), q_p, qit_p, wt_p, ki_all, kt_all, v_all, wo,
                         tq=tq, pos0=past_len, l_true=l_true, n_valid=t)[:, :t]
    return (y, kf.reshape(b, t, N_KV_HEADS, HEAD_DIM),
            vf.reshape(b, t, N_KV_HEADS, HEAD_DIM), kif.reshape(b, t, IDX_DIM))


def _gmlp_layer(x, g_mix, w_in, v_g, w_s, b_s, w_out, *, tm, emit_v):
    b, t, _ = x.shape
    n = b * t
    tc = min(t, GM_CHUNK)
    reps = GM_CHUNK // tc
    ws = jnp.tile(w_s[:, :tc, :tc], (1, reps, reps))
    bs = jnp.broadcast_to(jnp.tile(b_s[:, :tc], (1, reps))[:, :, None], (GM_GROUPS, GM_CHUNK, GM_CHUNK))
    y, v = _gmlp(x.reshape(n, D_MODEL), g_mix[None, :], w_in.astype(BF16), v_g[None, :], ws, bs,
                 w_out.astype(BF16), tm=tm, tc=tc, emit_v=emit_v)
    return y.reshape(b, t, D_MODEL), (v.reshape(b, t, D_MODEL) if emit_v else None)


def _ffn_ple_layer(x, p_all, layer, past, g_ffn, w_up, conv_w, conv_b, w_down, g_ple, w_gate, w_proj, *, tm):
    b, t, _ = x.shape
    n = b * t
    y, state = _conv_ffn_ple(x.reshape(n, D_MODEL), p_all.reshape(-1, p_all.shape[-1]), g_ffn[None, :],
                             w_up.astype(BF16), conv_w, conv_b[None, :], w_down.astype(BF16),
                             g_ple[None, :], w_gate.astype(BF16), w_proj.astype(BF16), past,
                             nb=b, tm=tm, layer=layer)
    return y.reshape(b, t, D_MODEL), state


def kernel(x_prompt, x_sample, cache_k, cache_v, cache_kidx, state_ffn_conv, p_prompt, p_sample,
           norm_mix, attn_w_in, attn_q_norm, attn_k_norm, attn_w_out,
           gmlp_w_in, gmlp_v_norm, gmlp_w_spatial, gmlp_b_spatial, gmlp_w_out,
           norm_ffn, ffn_w_up, ffn_conv_w, ffn_conv_b, ffn_w_down,
           norm_ple, ple_w_gate, ple_w_proj):
    depth = norm_mix.shape[0]
    t_s = x_sample.shape[1]
    n_s = x_sample.shape[0] * t_s
    yp, ys = x_prompt, x_sample
    kp, vp, kip, ks, vs, kis, gvs, cps, css = [], [], [], [], [], [], [], [], []
    for i in range(depth):
        j = i // 2
        if i % 2 == 0:
            yp, k, v, ki = _attn_layer(yp, norm_mix[i], attn_w_in[j], attn_q_norm[j], attn_k_norm[j],
                                       attn_w_out[j], None, tm=512, tq=256)
            kp.append(k); vp.append(v); kip.append(ki)
            ys, k, v, ki = _attn_layer(ys, norm_mix[i], attn_w_in[j], attn_q_norm[j], attn_k_norm[j],
                                       attn_w_out[j], (cache_k[j], cache_v[j], cache_kidx[j]), tm=n_s, tq=128)
            ks.append(k); vs.append(v); kis.append(ki)
        else:
            yp, _ = _gmlp_layer(yp, norm_mix[i], gmlp_w_in[j], gmlp_v_norm[j], gmlp_w_spatial[j],
                                gmlp_b_spatial[j], gmlp_w_out[j], tm=512, emit_v=False)
            ys, gv = _gmlp_layer(ys, norm_mix[i], gmlp_w_in[j], gmlp_v_norm[j], gmlp_w_spatial[j],
                                 gmlp_b_spatial[j], gmlp_w_out[j], tm=n_s, emit_v=True)
            gvs.append(gv)
        ffn = (norm_ffn[i], ffn_w_up[i], ffn_conv_w[i], ffn_conv_b[i], ffn_w_down[i],
               norm_ple[i], ple_w_gate[i], ple_w_proj[i])
        yp, cp = _ffn_ple_layer(yp, p_prompt, i, None, *ffn, tm=512)
        ys, cs = _ffn_ple_layer(ys, p_sample, i, state_ffn_conv[i], *ffn, tm=t_s)
        cps.append(cp); css.append(cs)
    return (yp, ys, jnp.stack(kp, 0), jnp.stack(vp, 0), jnp.stack(kip, 0),
            jnp.stack(ks, 0), jnp.stack(vs, 0), jnp.stack(kis, 0), jnp.stack(gvs, 0),
            jnp.stack(cps, 0), jnp.stack(css, 0))
```

```python
import functools

import numpy as np
import jax
import jax.numpy as jnp
from jax import lax
from jax.experimental import pallas as pl
from jax.experimental.pallas import tpu as pltpu

F32 = jnp.float32
BF16 = jnp.bfloat16
I32 = jnp.int32
I16 = jnp.int16

D_MODEL = 1024
N_HEADS = 8
HEAD_DIM = 128
N_KV_HEADS = 2
N_IDX_HEADS = 8
IDX_DIM = 64
CHUNK = 64
TOPK_MAX = 256
ROPE_THETA = 500000.0
ROPE_FRAC_DIV = 4
GM_GROUPS = 8
GM_CHUNK = 128
D_FF = 2816
CONV_W = 3
EPS = 1e-6

O_Q = N_HEADS * HEAD_DIM
O_K = O_Q + N_KV_HEADS * HEAD_DIM
O_V = O_K + N_KV_HEADS * HEAD_DIM
O_QI = O_V + N_IDX_HEADS * IDX_DIM
O_KI = O_QI + IDX_DIM
ATTN_PROJ = O_KI + N_IDX_HEADS

LANES = 128
ATTN_PROJ_PAD = 2176
INT_MIN = -2147483648
I16_MIN = -32768
MASK_BIAS = -1e30
LOGIT_SCALE = (HEAD_DIM ** -0.5) * 1.4426950408889634
KEY_TILE = 256
COUNT_ROWS = 64
Q_ROWS = 128
ATTN_UNROLL = 4
VMEM_LIMIT = 56 * 1024 * 1024


def _cparams(sem, flags=None):
    return pltpu.CompilerParams(dimension_semantics=sem, vmem_limit_bytes=VMEM_LIMIT, flags=flags)


def _nt_dot(a, b):
    return lax.dot_general(a, b, (((1,), (1,)), ((), ())), preferred_element_type=F32)


def _rms(x):
    return x * lax.rsqrt(jnp.mean(x * x, axis=-1, keepdims=True) + EPS)


def _rope_tables(pos, head_w):
    rot = head_w // ROPE_FRAC_DIV
    half = rot // 2
    inv = ROPE_THETA ** (-jnp.arange(half, dtype=F32) * (2.0 / rot))
    ang = pos.astype(F32)[:, None] * inv[None, :]
    cos, sin = jnp.cos(ang), jnp.sin(ang)
    j = np.arange(LANES) % head_w
    idx = j % half
    first = jnp.asarray(j < half)[None, :]
    second = jnp.asarray((j >= half) & (j < rot))[None, :]
    cos_t = jnp.where(first | second, cos[:, idx], 1.0)
    sin_t = jnp.where(first, -sin[:, idx], jnp.where(second, sin[:, idx], 0.0))
    return cos_t.astype(F32), sin_t.astype(F32)


T_Q, T_K, T_V, T_QI, T_KI, T_W = 0, 8, 10, 12, 16, 17
T_KP, T_QIP, T_KIP = 18, 20, 24
PROJ_TILES = 25
PROJ_COLS = PROJ_TILES * LANES
ROPE_HALF = HEAD_DIM // ROPE_FRAC_DIV // 2


def _partner_columns(base, width):
    j = np.arange(LANES)
    half = width // ROPE_FRAC_DIV // 2
    jj = j % width
    out = np.full(LANES, ATTN_PROJ, np.int64)
    out[jj < half] = (base + j + half)[jj < half]
    sec = (jj >= half) & (jj < 2 * half)
    out[sec] = (base + j - half)[sec]
    return out


def _proj_column_sources():
    src = np.full((PROJ_TILES, LANES), ATTN_PROJ, np.int64)
    j = np.arange(LANES)
    for t in range(T_KI):
        src[t] = t * LANES + j
    src[T_KI] = O_QI + j % IDX_DIM
    src[T_W, :N_IDX_HEADS] = O_KI + j[:N_IDX_HEADS]
    for t in range(N_KV_HEADS):
        src[T_KP + t] = _partner_columns(O_Q + t * HEAD_DIM, HEAD_DIM)
    for t in range(T_KI - T_QI):
        src[T_QIP + t] = _partner_columns(O_V + t * LANES, IDX_DIM)
    kip = _partner_columns(0, IDX_DIM)
    src[T_KIP] = np.where(kip == ATTN_PROJ, ATTN_PROJ, O_QI + kip % IDX_DIM)
    return src.reshape(-1)


def _attn_proj_body(x_ref, gm_ref, w_ref, qg_ref, kg_ref, kgp_ref, cq_ref, sq_ref, ci_ref, si_ref,
                    q_ref, kf_ref, vf_ref, kif_ref, kt_ref, vb_ref, kic_ref, qit_ref, wt_ref,
                    z_ref, *, tm):
    x = x_ref[...]
    h = _rms(x) * gm_ref[...]
    z_ref[...] = jnp.dot(h.astype(BF16), w_ref[...], preferred_element_type=F32)

    rb = LANES
    w_scale = (N_IDX_HEADS ** -0.5) * (IDX_DIM ** -0.5)
    first_half = lax.broadcasted_iota(I32, (rb, LANES), 1) < ROPE_HALF

    def tile(rows, t):
        return z_ref[rows, t * LANES:(t + 1) * LANES]

    def normed_rope(rows, t, tp, g_ref, gp_ref, c, s):
        z = tile(rows, t)
        rs = lax.rsqrt(jnp.mean(z * z, axis=-1, keepdims=True) + EPS)
        y = z * rs * g_ref[...]
        if tp is None:
            yp = jnp.where(first_half, pltpu.roll(y, LANES - ROPE_HALF, 1), pltpu.roll(y, ROPE_HALF, 1))
        else:
            yp = tile(rows, tp) * rs * gp_ref[...]
        return y * c + yp * s

    def split(y):
        hi = y.astype(BF16).astype(F32)
        return hi, y - hi

    for r in range(tm // rb):
        rows = slice(r * rb, (r + 1) * rb)
        cq, sq = cq_ref[rows, :], sq_ref[rows, :]
        ci, si = ci_ref[rows, :], si_ref[rows, :]
        for hd in range(N_HEADS):
            cols = slice(hd * HEAD_DIM, (hd + 1) * HEAD_DIM)
            y = normed_rope(rows, T_Q + hd, None, qg_ref, None, cq, sq)
            q_ref[rows, cols] = (y * LOGIT_SCALE).astype(BF16)
        for g in range(N_KV_HEADS):
            cols = slice(g * HEAD_DIM, (g + 1) * HEAD_DIM)
            y = normed_rope(rows, T_K + g, T_KP + g, kg_ref, kgp_ref, cq, sq)
            kf_ref[rows, g, :] = y
            kt_ref[r, cols, :] = y.T.astype(BF16)
            v = tile(rows, T_V + g)
            vf_ref[rows, g, :] = v
            vb_ref[rows, cols] = v.astype(BF16)
        for t in range(T_KI - T_QI):
            y = tile(rows, T_QI + t) * ci + tile(rows, T_QIP + t) * si
            hi, lo = split(y)
            hi_t, lo_t = hi.T.astype(BF16), lo.T.astype(BF16)
            for u in range(2):
                base = (2 * t + u) * LANES
                dims = slice(u * IDX_DIM, (u + 1) * IDX_DIM)
                qit_ref[r, base:base + IDX_DIM, :] = hi_t[dims, :]
                qit_ref[r, base + IDX_DIM:base + LANES, :] = lo_t[dims, :]
        y = tile(rows, T_KI) * ci + tile(rows, T_KIP) * si
        kif_ref[rows, :] = y[:, :IDX_DIM]
        hi, lo = split(y)
        kic_ref[rows, 0:LANES] = hi.astype(BF16)
        kic_ref[rows, LANES:2 * LANES] = lo.astype(BF16)
        wt_ref[:, rows] = tile(rows, T_W).T[:N_IDX_HEADS, :] * w_scale


def _attn_proj(x, g_mix, w_in, gains, tabs, *, tm, n_tab):
    n = x.shape[0]
    cq, sq, ci, si = tabs
    row = lambda i: (i, 0)
    tab = lambda i: (i % n_tab, 0)
    const = lambda i: (0, 0)
    out_shape = (
        jax.ShapeDtypeStruct((n, O_Q), BF16),
        jax.ShapeDtypeStruct((n, N_KV_HEADS, HEAD_DIM), F32),
        jax.ShapeDtypeStruct((n, N_KV_HEADS, HEAD_DIM), F32),
        jax.ShapeDtypeStruct((n, IDX_DIM), F32),
        jax.ShapeDtypeStruct((n // LANES, N_KV_HEADS * HEAD_DIM, LANES), BF16),
        jax.ShapeDtypeStruct((n, N_KV_HEADS * HEAD_DIM), BF16),
        jax.ShapeDtypeStruct((n, 2 * LANES), BF16),
        jax.ShapeDtypeStruct((n // LANES, N_IDX_HEADS * LANES, LANES), BF16),
        jax.ShapeDtypeStruct((N_IDX_HEADS, n), F32),
    )
    tile3 = lambda i: (i, 0, 0)
    rspec = lambda w: pl.BlockSpec((tm, w), row)
    kvspec = pl.BlockSpec((tm, N_KV_HEADS, HEAD_DIM), tile3)
    out_specs = (rspec(O_Q), kvspec, kvspec, rspec(IDX_DIM),
                 pl.BlockSpec((tm // LANES, N_KV_HEADS * HEAD_DIM, LANES), tile3),
                 rspec(256), rspec(2 * LANES),
                 pl.BlockSpec((tm // LANES, N_IDX_HEADS * LANES, LANES), tile3),
                 pl.BlockSpec((N_IDX_HEADS, tm), lambda i: (0, i)))
    return pl.pallas_call(
        functools.partial(_attn_proj_body, tm=tm),
        out_shape=out_shape,
        grid=(n // tm,),
        in_specs=[
            pl.BlockSpec((tm, D_MODEL), row),
            pl.BlockSpec((1, D_MODEL), const),
            pl.BlockSpec((D_MODEL, PROJ_COLS), const),
            pl.BlockSpec((1, HEAD_DIM), const),
            pl.BlockSpec((1, HEAD_DIM), const),
            pl.BlockSpec((1, HEAD_DIM), const),
            pl.BlockSpec((tm, LANES), tab),
            pl.BlockSpec((tm, LANES), tab),
            pl.BlockSpec((tm, LANES), tab),
            pl.BlockSpec((tm, LANES), tab),
        ],
        out_specs=out_specs,
        scratch_shapes=[pltpu.VMEM((tm, PROJ_COLS), F32)],
        compiler_params=_cparams(("parallel",)),
        name="attn_proj",
    )(x, g_mix, w_in, *gains, cq, sq, ci, si)


def _sparse_attn_body(x_ref, q_ref, qit_ref, wt_ref, ki_ref, kt_ref, v_ref, wo_ref, y_ref,
                      keys_ref, hi_ref, lo_ref, bias_ref, sacc_ref, qs_ref, s_ref, p_ref, alpha_ref, m_ref,
                      acc_ref, o_ref, *, tq, tqa, pos0, l_true, n_valid, k_sel):
    j = pl.program_id(1)
    q0 = pos0 + j * tq
    nqt = tq // LANES
    lane_q = lax.broadcasted_iota(I32, (1, tq), 1)
    qpos = q0 + lane_q
    limit = jnp.minimum((lax.shift_right_logical(qpos, 6) + 1) * CHUNK, l_true)
    lim_max = jnp.minimum((lax.shift_right_logical(q0 + tq - 1, 6) + 1) * CHUNK, l_true)
    n_kt = lax.shift_right_logical(lim_max + KEY_TILE - 1, 8)

    def score_rows(r0, nrows):
        rows = pl.ds(r0, nrows)
        kt = ki_ref[0, rows, :]
        for h in range(N_IDX_HEADS):
            qh = jnp.concatenate([qit_ref[u, h * LANES:(h + 1) * LANES, :] for u in range(nqt)], axis=1)
            s = jnp.dot(kt, jnp.concatenate([qh, qh], axis=0), preferred_element_type=F32)
            term = wt_ref[h:h + 1, :] * jnp.maximum(s, 0.0)
            if h == 0:
                sacc_ref[0:nrows, :] = term
            elif h < N_IDX_HEADS - 1:
                sacc_ref[0:nrows, :] += term
            else:
                bits = pltpu.bitcast(sacc_ref[0:nrows, :] + term, I32)
                key = jnp.where(bits < 0, INT_MIN - bits, bits)
                key = jnp.where(r0 + lax.broadcasted_iota(I32, (nrows, tq), 0) < limit, key, INT_MIN)
                keys_ref[rows, :] = key
                hi_ref[rows, :] = lax.shift_right_arithmetic(key, 16).astype(I16)
                lo_ref[rows, :] = ((key & 0xFFFF) + I16_MIN).astype(I16)

    def score_pair(i, carry):
        score_rows(pl.multiple_of(i * 2 * KEY_TILE, 2 * KEY_TILE), 2 * KEY_TILE)
        return carry

    def score_last(i, carry):
        score_rows(pl.multiple_of((n_kt - 1) * KEY_TILE, KEY_TILE), KEY_TILE)
        return carry

    lax.fori_loop(0, lax.shift_right_logical(n_kt, 1), score_pair, 0)
    lax.fori_loop(0, n_kt & 1, score_last, 0)

    def count(pred):
        def body(t, acc):
            for u in range(KEY_TILE // COUNT_ROWS):
                r0 = pl.multiple_of(t * KEY_TILE + u * COUNT_ROWS, COUNT_ROWS)
                acc = acc + jnp.where(pred(keys_ref[pl.ds(r0, COUNT_ROWS), :], r0), 1.0, 0.0)
            return acc
        acc = lax.fori_loop(0, n_kt, body, jnp.zeros((COUNT_ROWS, tq), F32))
        return jnp.sum(acc, axis=0, keepdims=True)

    def count16(ref, cand):
        c16 = jnp.broadcast_to(cand, (COUNT_ROWS, tq)).astype(I16)

        def body(t, acc):
            for u in range(KEY_TILE // COUNT_ROWS):
                r0 = pl.multiple_of(t * KEY_TILE + u * COUNT_ROWS, COUNT_ROWS)
                acc = acc + jnp.where(ref[pl.ds(r0, COUNT_ROWS), :] >= c16, jnp.int16(1), jnp.int16(0))
            return acc
        acc = lax.fori_loop(0, n_kt, body, jnp.zeros((COUNT_ROWS, tq), I16))
        return jnp.sum(acc.astype(I32).astype(F32), axis=0, keepdims=True)

    def search16(ref, need):
        def step(p, ans):
            cand = ans + lax.shift_left(jnp.int32(1), 15 - p)
            return jnp.where(count16(ref, cand) >= need, cand, ans)
        return lax.fori_loop(0, 16, step, jnp.full((1, tq), I16_MIN, I32))

    kf = float(k_sel)
    hi_k = search16(hi_ref, kf)
    n_above = jnp.where(hi_k >= -I16_MIN - 1, 0.0, count16(hi_ref, hi_k + 1))
    hi_k16 = jnp.broadcast_to(hi_k, (KEY_TILE, tq)).astype(I16)

    def keep_lo(t, carry):
        r0 = pl.multiple_of(t * KEY_TILE, KEY_TILE)
        rows = pl.ds(r0, KEY_TILE)
        lo_ref[rows, :] = jnp.where(hi_ref[rows, :] == hi_k16, lo_ref[rows, :], jnp.int16(I16_MIN))
        return carry

    lax.fori_loop(0, n_kt, keep_lo, 0)
    lo_k = search16(lo_ref, kf - n_above)
    thr = hi_k * 65536 + (lo_k - I16_MIN)
    thr_c = jnp.maximum(thr, INT_MIN + 1)
    n_ge = count(lambda c, r0: c >= thr_c)

    def to_rows(v):
        b = jnp.broadcast_to(v, (LANES, tq))
        col = jnp.concatenate([b[:, u * LANES:(u + 1) * LANES].T for u in range(nqt)], axis=0)
        return jnp.concatenate([col] * (KEY_TILE // LANES), axis=1)

    def write_bias(selected):
        def body(t, carry):
            r0 = pl.multiple_of(t * KEY_TILE, KEY_TILE)
            kq = keys_ref[pl.ds(r0, KEY_TILE), :].T
            bias_ref[t] = jnp.where(selected(kq, r0), 0.0, MASK_BIAS)
            return carry
        lax.fori_loop(0, n_kt, body, 0)

    thr_rows = to_rows(thr_c)
    write_bias(lambda kq, r0: kq >= thr_rows)

    excess = jnp.where((n_ge > kf) & (lane_q < n_valid), 1.0, 0.0)

    @pl.when(jnp.max(excess) > 0.0)
    def _():
        need = kf - count(lambda c, r0: c > thr)
        idx_iota = lax.broadcasted_iota(I32, (COUNT_ROWS, tq), 0)
        last = jnp.zeros((1, tq), I32)
        for bit in range(13, -1, -1):
            cand = last | (1 << bit)
            cnt = count(lambda c, r0: (c == thr) & (r0 + idx_iota < cand))
            last = jnp.where(cnt < need, cand, last)
        key_pos = lax.broadcasted_iota(I32, (tq, KEY_TILE), 1)
        t_rows, l_rows = to_rows(thr), to_rows(last)
        write_bias(lambda kq, r0: (kq > t_rows) | ((kq == t_rows) & (kq > INT_MIN) & (r0 + key_pos <= l_rows)))

    hpg = N_HEADS // N_KV_HEADS
    qr = min(Q_ROWS, hpg * tqa)
    for g in range(N_KV_HEADS):
        qs_ref[g] = jnp.concatenate(
            [q_ref[0, :tqa, (g * hpg + hh) * HEAD_DIM:(g * hpg + hh + 1) * HEAD_DIM] for hh in range(hpg)], axis=0)
    m_ref[...] = jnp.full(m_ref.shape, MASK_BIAS, F32)
    acc_ref[...] = jnp.zeros(acc_ref.shape, F32)
    ones = jnp.ones((KEY_TILE, HEAD_DIM), BF16)

    def bias_rows(t, rb):
        if tqa >= qr:
            q0r = (rb % (tqa // qr)) * qr
            return bias_ref[t, q0r:q0r + qr, :]
        return jnp.concatenate([bias_ref[t, 0:tqa, :]] * (qr // tqa), axis=0)

    def attn_tile(t, slot):
        r0 = pl.multiple_of(t * KEY_TILE, KEY_TILE)
        for g in range(N_KV_HEADS):
            gcols = slice(g * HEAD_DIM, (g + 1) * HEAD_DIM)
            kt = jnp.concatenate([kt_ref[2 * t + u, gcols, :] for u in range(KEY_TILE // LANES)], axis=1)
            s_ref[slot] = jnp.dot(qs_ref[g], kt, preferred_element_type=F32)
            for rb in range(hpg * tqa // qr):
                rows = slice(rb * qr, (rb + 1) * qr)
                s = s_ref[slot, rows, :] + bias_rows(t, rb)
                s0, s1 = s[:, :LANES], s[:, LANES:]
                m_old = m_ref[g, rows, :]
                m_new = jnp.maximum(m_old, jnp.max(jnp.maximum(s0, s1), axis=-1, keepdims=True))
                alpha_ref[slot, rows, :] = jnp.exp2(m_old - m_new)
                m_ref[g, rows, :] = m_new
                p_ref[slot, rows, :LANES] = jnp.exp2(s0 - m_new).astype(BF16)
                p_ref[slot, rows, LANES:] = jnp.exp2(s1 - m_new).astype(BF16)
            vt = jnp.concatenate([v_ref[0, pl.ds(r0, KEY_TILE), gcols], ones], axis=1)
            a = alpha_ref[slot]
            acc_ref[g] = acc_ref[g] * jnp.concatenate([a, a], axis=1) + jnp.dot(
                p_ref[slot], vt, preferred_element_type=F32)

    def attn_group(first_tile, width):
        def body(i, carry):
            for u in range(width):
                attn_tile(first_tile + i * width + u, u)
            return carry
        return body

    n4 = lax.shift_right_logical(n_kt, 2)
    n2 = lax.shift_right_logical(n_kt, 1) & 1
    lax.fori_loop(0, n4, attn_group(0, ATTN_UNROLL), 0)
    lax.fori_loop(0, n2, attn_group(n4 * 4, 2), 0)
    lax.fori_loop(0, n_kt & 1, attn_group(n_kt - 1, 1), 0)
    for h in range(N_HEADS):
        g, hh = divmod(h, hpg)
        rows = slice(hh * tqa, (hh + 1) * tqa)
        o_ref[:, h * HEAD_DIM:(h + 1) * HEAD_DIM] = (
            acc_ref[g, rows, :HEAD_DIM] / acc_ref[g, rows, HEAD_DIM:]).astype(BF16)
    y_ref[0, :tqa, :] = x_ref[0, :tqa, :] + jnp.dot(o_ref[...], wo_ref[...], preferred_element_type=F32)
    if tqa < tq:
        y_ref[0, tqa:, :] = x_ref[0, tqa:, :]


def _sparse_attn(x, q, qit, wt, ki, kt, v, w_out, *, tq, pos0, l_true, n_valid):
    b, t_q, _ = q.shape
    lp = v.shape[1]
    nq = t_q // tq
    nqt = tq // LANES
    k_sel = min(TOPK_MAX, l_true // 4)
    hpg = N_HEADS // N_KV_HEADS
    tqa = tq
    ar = hpg * tqa
    body = functools.partial(_sparse_attn_body, tq=tq, tqa=tqa, pos0=pos0, l_true=l_true,
                             n_valid=n_valid, k_sel=k_sel)
    xspec = pl.BlockSpec((1, tq, D_MODEL), lambda i, j: (i, j, 0))
    qspec = pl.BlockSpec((1, tq, O_Q), lambda i, j: (i, j, 0))
    kspec = pl.BlockSpec((1, lp, 2 * LANES), lambda i, j: (i, 0, 0))
    return pl.pallas_call(
        body,
        out_shape=jax.ShapeDtypeStruct((b, t_q, D_MODEL), F32),
        grid=(b, nq),
        in_specs=[xspec, qspec,
                  pl.BlockSpec((nqt, N_IDX_HEADS * LANES, LANES), lambda i, j: (i * nq + j, 0, 0)),
                  pl.BlockSpec((N_IDX_HEADS, tq), lambda i, j: (0, i * nq + j)),
                  kspec,
                  pl.BlockSpec((lp // LANES, 2 * LANES, LANES), lambda i, j: (i, 0, 0)),
                  kspec,
                  pl.BlockSpec((O_Q, D_MODEL), lambda i, j: (0, 0))],
        out_specs=xspec,
        scratch_shapes=[pltpu.VMEM((lp, tq), I32), pltpu.VMEM((lp, tq), I16), pltpu.VMEM((lp, tq), I16),
                        pltpu.VMEM((lp // KEY_TILE, tq, KEY_TILE), F32),
                        pltpu.VMEM((2 * KEY_TILE, tq), F32),
                        pltpu.VMEM((N_KV_HEADS, ar, HEAD_DIM), BF16),
                        pltpu.VMEM((ATTN_UNROLL, ar, KEY_TILE), F32),
                        pltpu.VMEM((ATTN_UNROLL, ar, KEY_TILE), BF16),
                        pltpu.VMEM((ATTN_UNROLL, ar, LANES), F32),
                        pltpu.VMEM((N_KV_HEADS, ar, LANES), F32),
                        pltpu.VMEM((N_KV_HEADS, ar, 2 * HEAD_DIM), F32),
                        pltpu.VMEM((tqa, O_Q), BF16)],
        compiler_params=_cparams(("parallel", "arbitrary")),
        name="sparse_attn",
    )(x, q, qit, wt, ki, kt, v, w_out)


def _gmlp_body(x_ref, gm_ref, w_in_ref, vg_ref, ws_ref, bs_ref, w_out_ref, *rest, tm, tc, emit_v):
    if emit_v:
        y_ref, v_ref, z_ref, s_ref = rest
    else:
        y_ref, z_ref, s_ref = rest
    x = x_ref[...]
    h = _rms(x) * gm_ref[...]
    z_ref[...] = jax.nn.gelu(jnp.dot(h.astype(BF16), w_in_ref[...], preferred_element_type=F32))
    v = _rms(z_ref[:, D_MODEL:]) * vg_ref[...]
    if emit_v:
        v_ref[...] = v
    z_ref[:, D_MODEL:] = v
    gw = D_MODEL // GM_GROUPS
    r_i = lax.broadcasted_iota(I32, (GM_CHUNK, GM_CHUNK), 0)
    c_i = lax.broadcasted_iota(I32, (GM_CHUNK, GM_CHUNK), 1)
    keep = (r_i // tc == c_i // tc) & (c_i <= r_i)
    for g in range(GM_GROUPS):
        ws = jnp.where(keep, ws_ref[g], 0.0).astype(BF16)
        bias = bs_ref[g]
        cols = slice(g * gw, (g + 1) * gw)
        for ch in range(tm // GM_CHUNK):
            rows = slice(ch * GM_CHUNK, (ch + 1) * GM_CHUNK)
            vb = z_ref[rows, D_MODEL + g * gw:D_MODEL + (g + 1) * gw].astype(BF16)
            mixed = jnp.dot(ws, vb, preferred_element_type=F32) + bias
            s_ref[rows, cols] = (z_ref[rows, cols] * mixed).astype(BF16)
    y_ref[...] = x + jnp.dot(s_ref[...], w_out_ref[...], preferred_element_type=F32)


def _gmlp(x, g_mix, w_in, v_g, ws, bs, w_out, *, tm, tc, emit_v):
    n = x.shape[0]
    row = lambda i: (i, 0)
    const2 = lambda i: (0, 0)
    out_shape = [jax.ShapeDtypeStruct((n, D_MODEL), F32)]
    out_specs = [pl.BlockSpec((tm, D_MODEL), row)]
    if emit_v:
        out_shape.append(jax.ShapeDtypeStruct((n, D_MODEL), F32))
        out_specs.append(pl.BlockSpec((tm, D_MODEL), row))
    res = pl.pallas_call(
        functools.partial(_gmlp_body, tm=tm, tc=tc, emit_v=emit_v),
        out_shape=tuple(out_shape),
        grid=(n // tm,),
        in_specs=[pl.BlockSpec((tm, D_MODEL), row), pl.BlockSpec((1, D_MODEL), const2),
                  pl.BlockSpec((D_MODEL, 2 * D_MODEL), const2), pl.BlockSpec((1, D_MODEL), const2),
                  pl.BlockSpec((GM_GROUPS, GM_CHUNK, GM_CHUNK), lambda i: (0, 0, 0)),
                  pl.BlockSpec((GM_GROUPS, GM_CHUNK, GM_CHUNK), lambda i: (0, 0, 0)),
                  pl.BlockSpec((D_MODEL, D_MODEL), const2)],
        out_specs=tuple(out_specs),
        scratch_shapes=[pltpu.VMEM((tm, 2 * D_MODEL), F32), pltpu.VMEM((tm, D_MODEL), BF16)],
        compiler_params=_cparams(("parallel",)),
        name="gmlp",
    )(x, g_mix, w_in, v_g, ws, bs, w_out)
    return res if emit_v else (res[0], None)


HALO = 8


FFN_COLS = 256
FFN_ROWS = 128


def _ffn_body(*refs, tm, nt, has_past):
    if has_past:
        (x_ref, gn_ref, wup_ref, cw_ref, cb_ref, wd_ref, p_ref, gp_ref, wgate_ref, wproj_ref, past_ref,
         y_ref, st_ref, hb_ref, stage_ref, s_ref, car_ref) = refs
    else:
        (x_ref, gn_ref, wup_ref, cw_ref, cb_ref, wd_ref, p_ref, gp_ref, wgate_ref, wproj_ref,
         y_ref, st_ref, hb_ref, stage_ref, s_ref, car_ref) = refs
        past_ref = None
    first = lax.rem(pl.program_id(0), nt) == 0

    @pl.when(pl.program_id(0) == 0)
    def _():
        car_ref[...] = jnp.zeros(car_ref.shape, F32)

    x = x_ref[...]
    hb_ref[...] = (_rms(x) * gn_ref[...]).astype(BF16)
    rb = min(tm, FFN_ROWS)
    n_stage = D_FF // FFN_COLS

    def cols_of(k, br):
        return slice(br * D_FF + k * FFN_COLS, br * D_FF + (k + 1) * FFN_COLS)

    def up_stage(k):
        for br in range(2):
            cols = cols_of(k, br)
            if past_ref is None:
                init = jnp.zeros((CONV_W - 1, FFN_COLS), F32)
            else:
                init = past_ref[0, :, cols]
            stage_ref[k % 2, br, HALO - 2:HALO, :] = jnp.where(first, init, car_ref[0:CONV_W - 1, cols])
            stage_ref[k % 2, br, HALO:HALO + tm, :] = jnp.dot(
                hb_ref[...], wup_ref[:, cols], preferred_element_type=F32)

    def conv_stage(k):
        slot = k % 2
        for br in range(2):
            tail = stage_ref[slot, br, HALO + tm - 2:HALO + tm, :]
            car_ref[0:CONV_W - 1, cols_of(k, br)] = tail
            st_ref[0, :, cols_of(k, br)] = tail
        for r in range(tm // rb):
            for cc in range(FFN_COLS // LANES):
                lanes = slice(cc * LANES, (cc + 1) * LANES)

                def conv(br):
                    c0 = br * D_FF + k * FFN_COLS + cc * LANES
                    wcol = slice(c0, c0 + LANES)
                    r0 = HALO + r * rb
                    return (cb_ref[:, wcol]
                            + cw_ref[2:3, wcol] * stage_ref[slot, br, r0:r0 + rb, lanes]
                            + cw_ref[1:2, wcol] * stage_ref[slot, br, r0 - 1:r0 - 1 + rb, lanes]
                            + cw_ref[0:1, wcol] * stage_ref[slot, br, r0 - 2:r0 - 2 + rb, lanes])

                gate = conv(0)
                s_ref[k % 3, r * rb:(r + 1) * rb, lanes] = (gate * jax.nn.sigmoid(gate) * conv(1)).astype(BF16)

    def down_stage(k):
        part = jnp.dot(s_ref[k % 3], wd_ref[k * FFN_COLS:(k + 1) * FFN_COLS, :], preferred_element_type=F32)
        if k == 0:
            y_ref[...] = x + part
        else:
            y_ref[...] += part

    up_stage(0)
    for k in range(n_stage):
        if k + 1 < n_stage:
            up_stage(k + 1)
        conv_stage(k)
        if k > 0:
            down_stage(k - 1)
    down_stage(n_stage - 1)

    y = y_ref[...]
    h = (_rms(y) * gp_ref[...]).astype(BF16)
    gate = jax.nn.sigmoid(jnp.dot(h, wgate_ref[...], preferred_element_type=F32))
    proj = jnp.dot(p_ref[...].astype(BF16), wproj_ref[...], preferred_element_type=F32)
    y_ref[...] = y + gate * proj


def _conv_ffn_ple(x, p, g_norm, w_up, conv_w, conv_b, w_down, g_ple, w_gate, w_proj, past, *, nb, tm, layer):
    n = x.shape[0]
    pd = p.shape[1]
    nt = n // nb // tm
    has_past = past is not None
    row = lambda i: (i, 0)
    p_row = lambda i: (layer * (n // tm) + i, 0)
    const = lambda i: (0, 0)
    in_specs = [
        pl.BlockSpec((tm, D_MODEL), row),
        pl.BlockSpec((1, D_MODEL), const),
        pl.BlockSpec((D_MODEL, 2 * D_FF), const),
        pl.BlockSpec((CONV_W, 2 * D_FF), const),
        pl.BlockSpec((1, 2 * D_FF), const),
        pl.BlockSpec((D_FF, D_MODEL), const),
        pl.BlockSpec((tm, pd), p_row),
        pl.BlockSpec((1, D_MODEL), const),
        pl.BlockSpec((D_MODEL, D_MODEL), const),
        pl.BlockSpec((pd, D_MODEL), const),
    ]
    args = [x, g_norm, w_up, conv_w, conv_b, w_down, p, g_ple, w_gate, w_proj]
    if has_past:
        in_specs.append(pl.BlockSpec((1, CONV_W - 1, 2 * D_FF), lambda i: (i // nt, 0, 0)))
        args.append(past)
    y, tails = pl.pallas_call(
        functools.partial(_ffn_body, tm=tm, nt=nt, has_past=has_past),
        out_shape=(jax.ShapeDtypeStruct((n, D_MODEL), F32),
                   jax.ShapeDtypeStruct((n // tm, CONV_W - 1, 2 * D_FF), F32)),
        grid=(n // tm,),
        in_specs=in_specs,
        out_specs=(pl.BlockSpec((tm, D_MODEL), row),
                   pl.BlockSpec((1, CONV_W - 1, 2 * D_FF), lambda i: (i, 0, 0))),
        scratch_shapes=[pltpu.VMEM((tm, D_MODEL), BF16),
                        pltpu.VMEM((2, 2, tm + HALO, FFN_COLS), F32),
                        pltpu.VMEM((3, tm, FFN_COLS), BF16),
                        pltpu.VMEM((HALO, 2 * D_FF), F32)],
        compiler_params=_cparams(("arbitrary",)),
        name="conv_ffn_ple",
    )(*args)
    return y, tails.reshape(nb, nt, CONV_W - 1, 2 * D_FF)[:, nt - 1]


def _cache_prep_body(ck_ref, cv_ref, cki_ref, kt_ref, v_ref, ki_ref, *, tm):
    for r in range(tm // LANES):
        rows = slice(r * LANES, (r + 1) * LANES)
        for g in range(N_KV_HEADS):
            cols = slice(g * HEAD_DIM, (g + 1) * HEAD_DIM)
            kt_ref[r, cols, :] = ck_ref[rows, g, :].T.astype(BF16)
            v_ref[rows, cols] = cv_ref[rows, g, :].astype(BF16)
        ki = cki_ref[rows, :]
        hi = ki.astype(BF16).astype(F32)
        ki_ref[rows, :] = jnp.concatenate([hi, hi, ki - hi, ki - hi], axis=1).astype(BF16)


def _cache_prep(ck, cv, cki, *, tm):
    n = ck.shape[0]
    kvw = N_KV_HEADS * HEAD_DIM
    kv_in = pl.BlockSpec((tm, N_KV_HEADS, HEAD_DIM), lambda i: (i, 0, 0))
    row = lambda i: (i, 0)
    return pl.pallas_call(
        functools.partial(_cache_prep_body, tm=tm),
        out_shape=(jax.ShapeDtypeStruct((n // LANES, kvw, LANES), BF16),
                   jax.ShapeDtypeStruct((n, kvw), BF16),
                   jax.ShapeDtypeStruct((n, 2 * LANES), BF16)),
        grid=(n // tm,),
        in_specs=[kv_in, kv_in, pl.BlockSpec((tm, IDX_DIM), row)],
        out_specs=(pl.BlockSpec((tm // LANES, kvw, LANES), lambda i: (i, 0, 0)),
                   pl.BlockSpec((tm, kvw), row), pl.BlockSpec((tm, 2 * LANES), row)),
        compiler_params=_cparams(("parallel",)),
        name="cache_prep",
    )(ck, cv, cki)


def _pad_rows(a, rows):
    return jnp.pad(a, ((0, 0), (0, rows - a.shape[1]), (0, 0)))


def _attn_layer(x, g_mix, w_in, q_g, k_g, w_out, cache, *, tm, tq):
    b, t, _ = x.shape
    n = b * t
    past_len = 0 if cache is None else cache[0].shape[1]
    pos = past_len + jnp.arange(t, dtype=I32)
    reps = tm // t if tm > t else 1
    tabs = tuple(jnp.tile(a, (reps, 1)) for a in _rope_tables(pos, HEAD_DIM) + _rope_tables(pos, IDX_DIM))
    n_tab = max(t // tm, 1)
    src = _proj_column_sources()
    w_wide = jnp.take(jnp.pad(w_in.astype(BF16), ((0, 0), (0, 1))), src, axis=1)
    k_gp = jnp.take(jnp.pad(k_g, (0, ATTN_PROJ + 1 - HEAD_DIM)), _partner_columns(0, HEAD_DIM))
    gains = (q_g[None, :], k_g[None, :], k_gp[None, :])
    q, kf, vf, kif, kt, vb, kic, qit, wt = _attn_proj(
        x.reshape(n, D_MODEL), g_mix[None, :], w_wide, gains, tabs, tm=tm, n_tab=n_tab)
    kvw = N_KV_HEADS * HEAD_DIM
    wo = w_out.astype(BF16)
    if cache is None:
        y = _sparse_attn(x, q.reshape(b, t, O_Q), qit, wt, kic.reshape(b, t, kvw), kt, vb.reshape(b, t, kvw), wo,
                         tq=tq, pos0=0, l_true=t, n_valid=tq)
    else:
        ck, cv, cki = cache
        l_true = past_len + t
        lp = -(-l_true // KEY_TILE) * KEY_TILE
        n_new = lp - past_len
        kt_c, v_c, ki_c = _cache_prep(ck.reshape(b * past_len, N_KV_HEADS, HEAD_DIM),
                                      cv.reshape(b * past_len, N_KV_HEADS, HEAD_DIM),
                                      cki.reshape(b * past_len, IDX_DIM), tm=512)
        k_new = _pad_rows(kf.reshape(b, t, kvw).astype(BF16), n_new)
        kt_new = k_new.reshape(b, n_new // LANES, LANES, kvw).transpose(0, 1, 3, 2)
        kt_all = jnp.concatenate([kt_c.reshape(b, past_len // LANES, kvw, LANES), kt_new], 1)
        kt_all = kt_all.reshape(b * lp // LANES, kvw, LANES)
        v_all = jnp.concatenate([v_c.reshape(b, past_len, kvw), _pad_rows(vb.reshape(b, t, kvw), n_new)], 1)
        ki_all = jnp.concatenate([ki_c.reshape(b, past_len, kvw), _pad_rows(kic.reshape(b, t, kvw), n_new)], 1)
        q_p = _pad_rows(q.reshape(b, t, O_Q), tq)
        qit_p = jnp.pad(qit.reshape(O_Q, b, t), ((0, 0), (0, 0), (0, tq - t))).transpose(1, 0, 2)
        wt_p = jnp.pad(wt.reshape(N_IDX_HEADS, b, t), ((0, 0), (0, 0), (0, tq - t))).reshape(N_IDX_HEADS, b * tq)
        y = _sparse_attn(_pad_rows(x, tq), q_p, qit_p, wt_p, ki_all, kt_all, v_all, wo,
                         tq=tq, pos0=past_len, l_true=l_true, n_valid=t)[:, :t]
    return (y, kf.reshape(b, t, N_KV_HEADS, HEAD_DIM),
            vf.reshape(b, t, N_KV_HEADS, HEAD_DIM), kif.reshape(b, t, IDX_DIM))


def _gmlp_layer(x, g_mix, w_in, v_g, w_s, b_s, w_out, *, tm, emit_v):
    b, t, _ = x.shape
    n = b * t
    tc = min(t, GM_CHUNK)
    reps = GM_CHUNK // tc
    ws = jnp.tile(w_s[:, :tc, :tc], (1, reps, reps))
    bs = jnp.broadcast_to(jnp.tile(b_s[:, :tc], (1, reps))[:, :, None], (GM_GROUPS, GM_CHUNK, GM_CHUNK))
    y, v = _gmlp(x.reshape(n, D_MODEL), g_mix[None, :], w_in.astype(BF16), v_g[None, :], ws, bs,
                 w_out.astype(BF16), tm=tm, tc=tc, emit_v=emit_v)
    return y.reshape(b, t, D_MODEL), (v.reshape(b, t, D_MODEL) if emit_v else None)


def _ffn_ple_layer(x, p_all, layer, past, g_ffn, w_up, conv_w, conv_b, w_down, g_ple, w_gate, w_proj, *, tm):
    b, t, _ = x.shape
    n = b * t
    y, state = _conv_ffn_ple(x.reshape(n, D_MODEL), p_all.reshape(-1, p_all.shape[-1]), g_ffn[None, :],
                             w_up.astype(BF16), conv_w, conv_b[None, :], w_down.astype(BF16),
                             g_ple[None, :], w_gate.astype(BF16), w_proj.astype(BF16), past,
                             nb=b, tm=tm, layer=layer)
    return y.reshape(b, t, D_MODEL), state


def kernel(x_prompt, x_sample, cache_k, cache_v, cache_kidx, state_ffn_conv, p_prompt, p_sample,
           norm_mix, attn_w_in, attn_q_norm, attn_k_norm, attn_w_out,
           gmlp_w_in, gmlp_v_norm, gmlp_w_spatial, gmlp_b_spatial, gmlp_w_out,
           norm_ffn, ffn_w_up, ffn_conv_w, ffn_conv_b, ffn_w_down,
           norm_ple, ple_w_gate, ple_w_proj):
    depth = norm_mix.shape[0]
    t_s = x_sample.shape[1]
    n_s = x_sample.shape[0] * t_s
    yp, ys = x_prompt, x_sample
    kp, vp, kip, ks, vs, kis, gvs, cps, css = [], [], [], [], [], [], [], [], []
    for i in range(depth):
        j = i // 2
        if i % 2 == 0:
            yp, k, v, ki = _attn_layer(yp, norm_mix[i], attn_w_in[j], attn_q_norm[j], attn_k_norm[j],
                                       attn_w_out[j], None, tm=512, tq=256)
            kp.append(k); vp.append(v); kip.append(ki)
            ys, k, v, ki = _attn_layer(ys, norm_mix[i], attn_w_in[j], attn_q_norm[j], attn_k_norm[j],
                                       attn_w_out[j], (cache_k[j], cache_v[j], cache_kidx[j]), tm=n_s, tq=128)
            ks.append(k); vs.append(v); kis.append(ki)
        else:
            yp, _ = _gmlp_layer(yp, norm_mix[i], gmlp_w_in[j], gmlp_v_norm[j], gmlp_w_spatial[j],
                                gmlp_b_spatial[j], gmlp_w_out[j], tm=512, emit_v=False)
            ys, gv = _gmlp_layer(ys, norm_mix[i], gmlp_w_in[j], gmlp_v_norm[j], gmlp_w_spatial[j],
                                 gmlp_b_spatial[j], gmlp_w_out[j], tm=n_s, emit_v=True)
            gvs.append(gv)
        ffn = (norm_ffn[i], ffn_w_up[i], ffn_conv_w[i], ffn_conv_b[i], ffn_w_down[i],
               norm_ple[i], ple_w_gate[i], ple_w_proj[i])
        yp, cp = _ffn_ple_layer(yp, p_prompt, i, None, *ffn, tm=512)
        ys, cs = _ffn_ple_layer(ys, p_sample, i, state_ffn_conv[i], *ffn, tm=t_s)
        cps.append(cp); css.append(cs)
    return (yp, ys, jnp.stack(kp, 0), jnp.stack(vp, 0), jnp.stack(kip, 0),
            jnp.stack(ks, 0), jnp.stack(vs, 0), jnp.stack(kis, 0), jnp.stack(gvs, 0),
            jnp.stack(cps, 0), jnp.stack(css, 0))
```

```python
import functools

import numpy as np
import jax
import jax.numpy as jnp
from jax import lax
from jax.experimental import pallas as pl
from jax.experimental.pallas import tpu as pltpu

F32 = jnp.float32
BF16 = jnp.bfloat16
I32 = jnp.int32
I16 = jnp.int16

D_MODEL = 1024
N_HEADS = 8
HEAD_DIM = 128
N_KV_HEADS = 2
N_IDX_HEADS = 8
IDX_DIM = 64
CHUNK = 64
TOPK_MAX = 256
ROPE_THETA = 500000.0
ROPE_FRAC_DIV = 4
GM_GROUPS = 8
GM_CHUNK = 128
D_FF = 2816
CONV_W = 3
EPS = 1e-6

O_Q = N_HEADS * HEAD_DIM
O_K = O_Q + N_KV_HEADS * HEAD_DIM
O_V = O_K + N_KV_HEADS * HEAD_DIM
O_QI = O_V + N_IDX_HEADS * IDX_DIM
O_KI = O_QI + IDX_DIM
ATTN_PROJ = O_KI + N_IDX_HEADS

LANES = 128
ATTN_PROJ_PAD = 2176
INT_MIN = -2147483648
I16_MIN = -32768
MASK_BIAS = -1e30
LOGIT_SCALE = (HEAD_DIM ** -0.5) * 1.4426950408889634
KEY_TILE = 256
COUNT_ROWS = 64
Q_ROWS = 128
ATTN_UNROLL = 4
VMEM_LIMIT = 56 * 1024 * 1024


def _cparams(sem, flags=None):
    return pltpu.CompilerParams(dimension_semantics=sem, vmem_limit_bytes=VMEM_LIMIT, flags=flags)


def _nt_dot(a, b):
    return lax.dot_general(a, b, (((1,), (1,)), ((), ())), preferred_element_type=F32)


def _rms(x):
    return x * lax.rsqrt(jnp.mean(x * x, axis=-1, keepdims=True) + EPS)


def _rope_tables(pos, head_w):
    rot = head_w // ROPE_FRAC_DIV
    half = rot // 2
    inv = ROPE_THETA ** (-jnp.arange(half, dtype=F32) * (2.0 / rot))
    ang = pos.astype(F32)[:, None] * inv[None, :]
    cos, sin = jnp.cos(ang), jnp.sin(ang)
    j = np.arange(LANES) % head_w
    idx = j % half
    first = jnp.asarray(j < half)[None, :]
    second = jnp.asarray((j >= half) & (j < rot))[None, :]
    cos_t = jnp.where(first | second, cos[:, idx], 1.0)
    sin_t = jnp.where(first, -sin[:, idx], jnp.where(second, sin[:, idx], 0.0))
    return cos_t.astype(F32), sin_t.astype(F32)


T_Q, T_K, T_V, T_QI, T_KI, T_W = 0, 8, 10, 12, 16, 17
T_KP, T_QIP, T_KIP = 18, 20, 24
PROJ_TILES = 25
PROJ_COLS = PROJ_TILES * LANES
ROPE_HALF = HEAD_DIM // ROPE_FRAC_DIV // 2


def _partner_columns(base, width):
    j = np.arange(LANES)
    half = width // ROPE_FRAC_DIV // 2
    jj = j % width
    out = np.full(LANES, ATTN_PROJ, np.int64)
    out[jj < half] = (base + j + half)[jj < half]
    sec = (jj >= half) & (jj < 2 * half)
    out[sec] = (base + j - half)[sec]
    return out


def _proj_column_sources():
    src = np.full((PROJ_TILES, LANES), ATTN_PROJ, np.int64)
    j = np.arange(LANES)
    for t in range(T_KI):
        src[t] = t * LANES + j
    src[T_KI] = O_QI + j % IDX_DIM
    src[T_W, :N_IDX_HEADS] = O_KI + j[:N_IDX_HEADS]
    for t in range(N_KV_HEADS):
        src[T_KP + t] = _partner_columns(O_Q + t * HEAD_DIM, HEAD_DIM)
    for t in range(T_KI - T_QI):
        src[T_QIP + t] = _partner_columns(O_V + t * LANES, IDX_DIM)
    kip = _partner_columns(0, IDX_DIM)
    src[T_KIP] = np.where(kip == ATTN_PROJ, ATTN_PROJ, O_QI + kip % IDX_DIM)
    return src.reshape(-1)


def _attn_proj_body(x_ref, gm_ref, w_ref, qg_ref, kg_ref, kgp_ref, cq_ref, sq_ref, ci_ref, si_ref,
                    q_ref, kf_ref, vf_ref, kif_ref, kt_ref, vb_ref, kic_ref, qit_ref, wt_ref,
                    z_ref, *, tm):
    x = x_ref[...]
    h = _rms(x) * gm_ref[...]
    z_ref[...] = jnp.dot(h.astype(BF16), w_ref[...], preferred_element_type=F32)

    rb = LANES
    w_scale = (N_IDX_HEADS ** -0.5) * (IDX_DIM ** -0.5)
    first_half = lax.broadcasted_iota(I32, (rb, LANES), 1) < ROPE_HALF

    def tile(rows, t):
        return z_ref[rows, t * LANES:(t + 1) * LANES]

    def normed_rope(rows, t, tp, g_ref, gp_ref, c, s):
        z = tile(rows, t)
        rs = lax.rsqrt(jnp.mean(z * z, axis=-1, keepdims=True) + EPS)
        y = z * rs * g_ref[...]
        if tp is None:
            yp = jnp.where(first_half, pltpu.roll(y, LANES - ROPE_HALF, 1), pltpu.roll(y, ROPE_HALF, 1))
        else:
            yp = tile(rows, tp) * rs * gp_ref[...]
        return y * c + yp * s

    def split(y):
        hi = y.astype(BF16).astype(F32)
        return hi, y - hi

    for r in range(tm // rb):
        rows = slice(r * rb, (r + 1) * rb)
        cq, sq = cq_ref[rows, :], sq_ref[rows, :]
        ci, si = ci_ref[rows, :], si_ref[rows, :]
        for hd in range(N_HEADS):
            cols = slice(hd * HEAD_DIM, (hd + 1) * HEAD_DIM)
            y = normed_rope(rows, T_Q + hd, None, qg_ref, None, cq, sq)
            q_ref[rows, cols] = (y * LOGIT_SCALE).astype(BF16)
        for g in range(N_KV_HEADS):
            cols = slice(g * HEAD_DIM, (g + 1) * HEAD_DIM)
            y = normed_rope(rows, T_K + g, T_KP + g, kg_ref, kgp_ref, cq, sq)
            kf_ref[rows, g, :] = y
            kt_ref[r, cols, :] = y.T.astype(BF16)
            v = tile(rows, T_V + g)
            vf_ref[rows, g, :] = v
            vb_ref[rows, cols] = v.astype(BF16)
        for t in range(T_KI - T_QI):
            y = tile(rows, T_QI + t) * ci + tile(rows, T_QIP + t) * si
            hi, lo = split(y)
            hi_t, lo_t = hi.T.astype(BF16), lo.T.astype(BF16)
            for u in range(2):
                base = (2 * t + u) * LANES
                dims = slice(u * IDX_DIM, (u + 1) * IDX_DIM)
                qit_ref[r, base:base + IDX_DIM, :] = hi_t[dims, :]
                qit_ref[r, base + IDX_DIM:base + LANES, :] = lo_t[dims, :]
        y = tile(rows, T_KI) * ci + tile(rows, T_KIP) * si
        kif_ref[rows, :] = y[:, :IDX_DIM]
        hi, lo = split(y)
        kic_ref[rows, 0:LANES] = hi.astype(BF16)
        kic_ref[rows, LANES:2 * LANES] = lo.astype(BF16)
        wt_ref[:, rows] = tile(rows, T_W).T[:N_IDX_HEADS, :] * w_scale


def _attn_proj(x, g_mix, w_in, gains, tabs, *, tm, n_tab):
    n = x.shape[0]
    cq, sq, ci, si = tabs
    row = lambda i: (i, 0)
    tab = lambda i: (i % n_tab, 0)
    const = lambda i: (0, 0)
    out_shape = (
        jax.ShapeDtypeStruct((n, O_Q), BF16),
        jax.ShapeDtypeStruct((n, N_KV_HEADS, HEAD_DIM), F32),
        jax.ShapeDtypeStruct((n, N_KV_HEADS, HEAD_DIM), F32),
        jax.ShapeDtypeStruct((n, IDX_DIM), F32),
        jax.ShapeDtypeStruct((n // LANES, N_KV_HEADS * HEAD_DIM, LANES), BF16),
        jax.ShapeDtypeStruct((n, N_KV_HEADS * HEAD_DIM), BF16),
        jax.ShapeDtypeStruct((n, 2 * LANES), BF16),
        jax.ShapeDtypeStruct((n // LANES, N_IDX_HEADS * LANES, LANES), BF16),
        jax.ShapeDtypeStruct((N_IDX_HEADS, n), F32),
    )
    tile3 = lambda i: (i, 0, 0)
    rspec = lambda w: pl.BlockSpec((tm, w), row)
    kvspec = pl.BlockSpec((tm, N_KV_HEADS, HEAD_DIM), tile3)
    out_specs = (rspec(O_Q), kvspec, kvspec, rspec(IDX_DIM),
                 pl.BlockSpec((tm // LANES, N_KV_HEADS * HEAD_DIM, LANES), tile3),
                 rspec(256), rspec(2 * LANES),
                 pl.BlockSpec((tm // LANES, N_IDX_HEADS * LANES, LANES), tile3),
                 pl.BlockSpec((N_IDX_HEADS, tm), lambda i: (0, i)))
    return pl.pallas_call(
        functools.partial(_attn_proj_body, tm=tm),
        out_shape=out_shape,
        grid=(n // tm,),
        in_specs=[
            pl.BlockSpec((tm, D_MODEL), row),
            pl.BlockSpec((1, D_MODEL), const),
            pl.BlockSpec((D_MODEL, PROJ_COLS), const),
            pl.BlockSpec((1, HEAD_DIM), const),
            pl.BlockSpec((1, HEAD_DIM), const),
            pl.BlockSpec((1, HEAD_DIM), const),
            pl.BlockSpec((tm, LANES), tab),
            pl.BlockSpec((tm, LANES), tab),
            pl.BlockSpec((tm, LANES), tab),
            pl.BlockSpec((tm, LANES), tab),
        ],
        out_specs=out_specs,
        scratch_shapes=[pltpu.VMEM((tm, PROJ_COLS), F32)],
        compiler_params=_cparams(("parallel",)),
        name="attn_proj",
    )(x, g_mix, w_in, *gains, cq, sq, ci, si)


def _sparse_attn_body(x_ref, q_ref, qit_ref, wt_ref, ki_ref, kt_ref, v_ref, wo_ref, y_ref,
                      keys_ref, hi_ref, lo_ref, bias_ref, sacc_ref, qs_ref, s_ref, p_ref, alpha_ref, m_ref,
                      acc_ref, o_ref, *, tq, pos0, l_true, n_valid, k_sel):
    j = pl.program_id(1)
    q0 = pos0 + j * tq
    nqt = tq // LANES
    lane_q = lax.broadcasted_iota(I32, (1, tq), 1)
    qpos = q0 + lane_q
    limit = jnp.minimum((lax.shift_right_logical(qpos, 6) + 1) * CHUNK, l_true)
    lim_max = jnp.minimum((lax.shift_right_logical(q0 + tq - 1, 6) + 1) * CHUNK, l_true)
    n_kt = lax.shift_right_logical(lim_max + KEY_TILE - 1, 8)

    def score_rows(r0, nrows):
        rows = pl.ds(r0, nrows)
        kt = ki_ref[0, rows, :]
        for h in range(N_IDX_HEADS):
            qh = jnp.concatenate([qit_ref[u, h * LANES:(h + 1) * LANES, :] for u in range(nqt)], axis=1)
            s = jnp.dot(kt, jnp.concatenate([qh, qh], axis=0), preferred_element_type=F32)
            term = wt_ref[h:h + 1, :] * jnp.maximum(s, 0.0)
            if h == 0:
                sacc_ref[0:nrows, :] = term
            elif h < N_IDX_HEADS - 1:
                sacc_ref[0:nrows, :] += term
            else:
                bits = pltpu.bitcast(sacc_ref[0:nrows, :] + term, I32)
                key = jnp.where(bits < 0, INT_MIN - bits, bits)
                key = jnp.where(r0 + lax.broadcasted_iota(I32, (nrows, tq), 0) < limit, key, INT_MIN)
                keys_ref[rows, :] = key
                hi_ref[rows, :] = lax.shift_right_arithmetic(key, 16).astype(I16)
                lo_ref[rows, :] = ((key & 0xFFFF) + I16_MIN).astype(I16)

    def score_pair(i, carry):
        score_rows(pl.multiple_of(i * 2 * KEY_TILE, 2 * KEY_TILE), 2 * KEY_TILE)
        return carry

    def score_last(i, carry):
        score_rows(pl.multiple_of((n_kt - 1) * KEY_TILE, KEY_TILE), KEY_TILE)
        return carry

    lax.fori_loop(0, lax.shift_right_logical(n_kt, 1), score_pair, 0)
    lax.fori_loop(0, n_kt & 1, score_last, 0)

    def count(pred):
        def body(t, acc):
            for u in range(KEY_TILE // COUNT_ROWS):
                r0 = pl.multiple_of(t * KEY_TILE + u * COUNT_ROWS, COUNT_ROWS)
                acc = acc + jnp.where(pred(keys_ref[pl.ds(r0, COUNT_ROWS), :], r0), 1.0, 0.0)
            return acc
        acc = lax.fori_loop(0, n_kt, body, jnp.zeros((COUNT_ROWS, tq), F32))
        return jnp.sum(acc, axis=0, keepdims=True)

    def count16(ref, cand):
        c16 = jnp.broadcast_to(cand, (COUNT_ROWS, tq)).astype(I16)

        def body(t, acc):
            for u in range(KEY_TILE // COUNT_ROWS):
                r0 = pl.multiple_of(t * KEY_TILE + u * COUNT_ROWS, COUNT_ROWS)
                acc = acc + jnp.where(ref[pl.ds(r0, COUNT_ROWS), :] >= c16, jnp.int16(1), jnp.int16(0))
            return acc
        acc = lax.fori_loop(0, n_kt, body, jnp.zeros((COUNT_ROWS, tq), I16))
        return jnp.sum(acc.astype(I32).astype(F32), axis=0, keepdims=True)

    def search16(ref, need):
        def step(p, ans):
            cand = ans + lax.shift_left(jnp.int32(1), 15 - p)
            return jnp.where(count16(ref, cand) >= need, cand, ans)
        return lax.fori_loop(0, 16, step, jnp.full((1, tq), I16_MIN, I32))

    kf = float(k_sel)
    hi_k = search16(hi_ref, kf)
    n_above = jnp.where(hi_k >= -I16_MIN - 1, 0.0, count16(hi_ref, hi_k + 1))
    hi_k16 = jnp.broadcast_to(hi_k, (KEY_TILE, tq)).astype(I16)

    def keep_lo(t, carry):
        r0 = pl.multiple_of(t * KEY_TILE, KEY_TILE)
        rows = pl.ds(r0, KEY_TILE)
        lo_ref[rows, :] = jnp.where(hi_ref[rows, :] == hi_k16, lo_ref[rows, :], jnp.int16(I16_MIN))
        return carry

    lax.fori_loop(0, n_kt, keep_lo, 0)
    lo_k = search16(lo_ref, kf - n_above)
    thr = hi_k * 65536 + (lo_k - I16_MIN)
    thr_c = jnp.maximum(thr, INT_MIN + 1)
    n_ge = count(lambda c, r0: c >= thr_c)

    def to_rows(v):
        b = jnp.broadcast_to(v, (LANES, tq))
        col = jnp.concatenate([b[:, u * LANES:(u + 1) * LANES].T for u in range(nqt)], axis=0)
        return jnp.concatenate([col] * (KEY_TILE // LANES), axis=1)

    def write_bias(selected):
        def tile(t):
            r0 = pl.multiple_of(t * KEY_TILE, KEY_TILE)
            kq = keys_ref[pl.ds(r0, KEY_TILE), :].T
            bias_ref[t] = jnp.where(selected(kq, r0), 0.0, MASK_BIAS)

        def pair(i, carry):
            tile(2 * i)
            tile(2 * i + 1)
            return carry

        def last(i, carry):
            tile(n_kt - 1)
            return carry

        lax.fori_loop(0, lax.shift_right_logical(n_kt, 1), pair, 0)
        lax.fori_loop(0, n_kt & 1, last, 0)

    thr_rows = to_rows(thr_c)
    write_bias(lambda kq, r0: kq >= thr_rows)

    excess = jnp.where((n_ge > kf) & (lane_q < n_valid), 1.0, 0.0)

    @pl.when(jnp.max(excess) > 0.0)
    def _():
        need = kf - count(lambda c, r0: c > thr)
        idx_iota = lax.broadcasted_iota(I32, (COUNT_ROWS, tq), 0)
        last = jnp.zeros((1, tq), I32)
        for bit in range(13, -1, -1):
            cand = last | (1 << bit)
            cnt = count(lambda c, r0: (c == thr) & (r0 + idx_iota < cand))
            last = jnp.where(cnt < need, cand, last)
        key_pos = lax.broadcasted_iota(I32, (tq, KEY_TILE), 1)
        t_rows, l_rows = to_rows(thr), to_rows(last)
        write_bias(lambda kq, r0: (kq > t_rows) | ((kq == t_rows) & (kq > INT_MIN) & (r0 + key_pos <= l_rows)))

    hpg = N_HEADS // N_KV_HEADS
    for g in range(N_KV_HEADS):
        qs_ref[g] = jnp.concatenate(
            [q_ref[0, :, (g * hpg + hh) * HEAD_DIM:(g * hpg + hh + 1) * HEAD_DIM] for hh in range(hpg)], axis=0)
    m_ref[...] = jnp.full(m_ref.shape, MASK_BIAS, F32)
    acc_ref[...] = jnp.zeros(acc_ref.shape, F32)
    ones = jnp.ones((KEY_TILE, HEAD_DIM), BF16)

    def attn_tile(t, slot):
        r0 = pl.multiple_of(t * KEY_TILE, KEY_TILE)
        for g in range(N_KV_HEADS):
            gcols = slice(g * HEAD_DIM, (g + 1) * HEAD_DIM)
            kt = jnp.concatenate([kt_ref[2 * t + u, gcols, :] for u in range(KEY_TILE // LANES)], axis=1)
            s_ref[slot] = jnp.dot(qs_ref[g], kt, preferred_element_type=F32)
            for rb in range(hpg * nqt):
                rows = slice(rb * Q_ROWS, (rb + 1) * Q_ROWS)
                qrows = slice((rb % nqt) * Q_ROWS, (rb % nqt + 1) * Q_ROWS)
                s = s_ref[slot, rows, :] + bias_ref[t, qrows, :]
                s0, s1 = s[:, :LANES], s[:, LANES:]
                m_old = m_ref[g, rows, :]
                m_new = jnp.maximum(m_old, jnp.max(jnp.maximum(s0, s1), axis=-1, keepdims=True))
                alpha_ref[slot, rows, :] = jnp.exp2(m_old - m_new)
                m_ref[g, rows, :] = m_new
                p_ref[slot, rows, :LANES] = jnp.exp2(s0 - m_new).astype(BF16)
                p_ref[slot, rows, LANES:] = jnp.exp2(s1 - m_new).astype(BF16)
            vt = jnp.concatenate([v_ref[0, pl.ds(r0, KEY_TILE), gcols], ones], axis=1)
            a = alpha_ref[slot]
            acc_ref[g] = acc_ref[g] * jnp.concatenate([a, a], axis=1) + jnp.dot(
                p_ref[slot], vt, preferred_element_type=F32)

    def attn_group(first_tile, width):
        def body(i, carry):
            for u in range(width):
                attn_tile(first_tile + i * width + u, u)
            return carry
        return body

    n4 = lax.shift_right_logical(n_kt, 2)
    n2 = lax.shift_right_logical(n_kt, 1) & 1
    lax.fori_loop(0, n4, attn_group(0, ATTN_UNROLL), 0)
    lax.fori_loop(0, n2, attn_group(n4 * 4, 2), 0)
    lax.fori_loop(0, n_kt & 1, attn_group(n_kt - 1, 1), 0)
    for h in range(N_HEADS):
        g, hh = divmod(h, hpg)
        rows = slice(hh * tq, (hh + 1) * tq)
        o_ref[:, h * HEAD_DIM:(h + 1) * HEAD_DIM] = (
            acc_ref[g, rows, :HEAD_DIM] / acc_ref[g, rows, HEAD_DIM:]).astype(BF16)
    y_ref[0] = x_ref[0] + jnp.dot(o_ref[...], wo_ref[...], preferred_element_type=F32)


def _sparse_attn(x, q, qit, wt, ki, kt, v, w_out, *, tq, pos0, l_true, n_valid):
    b, t_q, _ = q.shape
    lp = v.shape[1]
    nq = t_q // tq
    nqt = tq // LANES
    k_sel = min(TOPK_MAX, l_true // 4)
    hpg = N_HEADS // N_KV_HEADS
    ar = hpg * tq
    body = functools.partial(_sparse_attn_body, tq=tq, pos0=pos0, l_true=l_true,
                             n_valid=n_valid, k_sel=k_sel)
    xspec = pl.BlockSpec((1, tq, D_MODEL), lambda i, j: (i, j, 0))
    qspec = pl.BlockSpec((1, tq, O_Q), lambda i, j: (i, j, 0))
    kspec = pl.BlockSpec((1, lp, 2 * LANES), lambda i, j: (i, 0, 0))
    return pl.pallas_call(
        body,
        out_shape=jax.ShapeDtypeStruct((b, t_q, D_MODEL), F32),
        grid=(b, nq),
        in_specs=[xspec, qspec,
                  pl.BlockSpec((nqt, N_IDX_HEADS * LANES, LANES), lambda i, j: (i * nq + j, 0, 0)),
                  pl.BlockSpec((N_IDX_HEADS, tq), lambda i, j: (0, i * nq + j)),
                  kspec,
                  pl.BlockSpec((lp // LANES, 2 * LANES, LANES), lambda i, j: (i, 0, 0)),
                  kspec,
                  pl.BlockSpec((O_Q, D_MODEL), lambda i, j: (0, 0))],
        out_specs=xspec,
        scratch_shapes=[pltpu.VMEM((lp, tq), I32), pltpu.VMEM((lp, tq), I16), pltpu.VMEM((lp, tq), I16),
                        pltpu.VMEM((lp // KEY_TILE, tq, KEY_TILE), F32),
                        pltpu.VMEM((2 * KEY_TILE, tq), F32),
                        pltpu.VMEM((N_KV_HEADS, ar, HEAD_DIM), BF16),
                        pltpu.VMEM((ATTN_UNROLL, ar, KEY_TILE), F32),
                        pltpu.VMEM((ATTN_UNROLL, ar, KEY_TILE), BF16),
                        pltpu.VMEM((ATTN_UNROLL, ar, LANES), F32),
                        pltpu.VMEM((N_KV_HEADS, ar, LANES), F32),
                        pltpu.VMEM((N_KV_HEADS, ar, 2 * HEAD_DIM), F32),
                        pltpu.VMEM((tq, O_Q), BF16)],
        compiler_params=_cparams(("parallel", "arbitrary")),
        name="sparse_attn",
    )(x, q, qit, wt, ki, kt, v, w_out)


def _gmlp_body(x_ref, gm_ref, w_in_ref, vg_ref, ws_ref, bs_ref, w_out_ref, *rest, tm, tc, emit_v):
    if emit_v:
        y_ref, v_ref, z_ref, s_ref = rest
    else:
        y_ref, z_ref, s_ref = rest
    x = x_ref[...]
    h = _rms(x) * gm_ref[...]
    z_ref[...] = jax.nn.gelu(jnp.dot(h.astype(BF16), w_in_ref[...], preferred_element_type=F32))
    v = _rms(z_ref[:, D_MODEL:]) * vg_ref[...]
    if emit_v:
        v_ref[...] = v
    z_ref[:, D_MODEL:] = v
    gw = D_MODEL // GM_GROUPS
    r_i = lax.broadcasted_iota(I32, (GM_CHUNK, GM_CHUNK), 0)
    c_i = lax.broadcasted_iota(I32, (GM_CHUNK, GM_CHUNK), 1)
    keep = (r_i // tc == c_i // tc) & (c_i <= r_i)
    for g in range(GM_GROUPS):
        ws = jnp.where(keep, ws_ref[g], 0.0).astype(BF16)
        bias = bs_ref[g]
        cols = slice(g * gw, (g + 1) * gw)
        for ch in range(tm // GM_CHUNK):
            rows = slice(ch * GM_CHUNK, (ch + 1) * GM_CHUNK)
            vb = z_ref[rows, D_MODEL + g * gw:D_MODEL + (g + 1) * gw].astype(BF16)
            mixed = jnp.dot(ws, vb, preferred_element_type=F32) + bias
            s_ref[rows, cols] = (z_ref[rows, cols] * mixed).astype(BF16)
    y_ref[...] = x + jnp.dot(s_ref[...], w_out_ref[...], preferred_element_type=F32)


def _gmlp(x, g_mix, w_in, v_g, ws, bs, w_out, *, tm, tc, emit_v):
    n = x.shape[0]
    row = lambda i: (i, 0)
    const2 = lambda i: (0, 0)
    out_shape = [jax.ShapeDtypeStruct((n, D_MODEL), F32)]
    out_specs = [pl.BlockSpec((tm, D_MODEL), row)]
    if emit_v:
        out_shape.append(jax.ShapeDtypeStruct((n, D_MODEL), F32))
        out_specs.append(pl.BlockSpec((tm, D_MODEL), row))
    res = pl.pallas_call(
        functools.partial(_gmlp_body, tm=tm, tc=tc, emit_v=emit_v),
        out_shape=tuple(out_shape),
        grid=(n // tm,),
        in_specs=[pl.BlockSpec((tm, D_MODEL), row), pl.BlockSpec((1, D_MODEL), const2),
                  pl.BlockSpec((D_MODEL, 2 * D_MODEL), const2), pl.BlockSpec((1, D_MODEL), const2),
                  pl.BlockSpec((GM_GROUPS, GM_CHUNK, GM_CHUNK), lambda i: (0, 0, 0)),
                  pl.BlockSpec((GM_GROUPS, GM_CHUNK, GM_CHUNK), lambda i: (0, 0, 0)),
                  pl.BlockSpec((D_MODEL, D_MODEL), const2)],
        out_specs=tuple(out_specs),
        scratch_shapes=[pltpu.VMEM((tm, 2 * D_MODEL), F32), pltpu.VMEM((tm, D_MODEL), BF16)],
        compiler_params=_cparams(("parallel",)),
        name="gmlp",
    )(x, g_mix, w_in, v_g, ws, bs, w_out)
    return res if emit_v else (res[0], None)


HALO = 8


FFN_COLS = 256
FFN_ROWS = 128


def _ffn_body(*refs, tm, nt, has_past):
    if has_past:
        (x_ref, gn_ref, wup_ref, cw_ref, cb_ref, wd_ref, p_ref, gp_ref, wgate_ref, wproj_ref, past_ref,
         y_ref, st_ref, hb_ref, stage_ref, s_ref, car_ref) = refs
    else:
        (x_ref, gn_ref, wup_ref, cw_ref, cb_ref, wd_ref, p_ref, gp_ref, wgate_ref, wproj_ref,
         y_ref, st_ref, hb_ref, stage_ref, s_ref, car_ref) = refs
        past_ref = None
    first = lax.rem(pl.program_id(0), nt) == 0

    @pl.when(pl.program_id(0) == 0)
    def _():
        car_ref[...] = jnp.zeros(car_ref.shape, F32)

    x = x_ref[...]
    hb_ref[...] = (_rms(x) * gn_ref[...]).astype(BF16)
    rb = min(tm, FFN_ROWS)
    n_stage = D_FF // FFN_COLS

    def cols_of(k, br):
        return slice(br * D_FF + k * FFN_COLS, br * D_FF + (k + 1) * FFN_COLS)

    def up_stage(k):
        for br in range(2):
            cols = cols_of(k, br)
            if past_ref is None:
                init = jnp.zeros((CONV_W - 1, FFN_COLS), F32)
            else:
                init = past_ref[0, :, cols]
            stage_ref[k % 2, br, HALO - 2:HALO, :] = jnp.where(first, init, car_ref[0:CONV_W - 1, cols])
            stage_ref[k % 2, br, HALO:HALO + tm, :] = jnp.dot(
                hb_ref[...], wup_ref[:, cols], preferred_element_type=F32)

    def conv_stage(k):
        slot = k % 2
        for br in range(2):
            tail = stage_ref[slot, br, HALO + tm - 2:HALO + tm, :]
            car_ref[0:CONV_W - 1, cols_of(k, br)] = tail
            st_ref[0, :, cols_of(k, br)] = tail
        for r in range(tm // rb):
            for cc in range(FFN_COLS // LANES):
                lanes = slice(cc * LANES, (cc + 1) * LANES)

                def conv(br):
                    c0 = br * D_FF + k * FFN_COLS + cc * LANES
                    wcol = slice(c0, c0 + LANES)
                    r0 = HALO + r * rb
                    return (cb_ref[:, wcol]
                            + cw_ref[2:3, wcol] * stage_ref[slot, br, r0:r0 + rb, lanes]
                            + cw_ref[1:2, wcol] * stage_ref[slot, br, r0 - 1:r0 - 1 + rb, lanes]
                            + cw_ref[0:1, wcol] * stage_ref[slot, br, r0 - 2:r0 - 2 + rb, lanes])

                gate = conv(0)
                s_ref[k % 3, r * rb:(r + 1) * rb, lanes] = (gate * jax.nn.sigmoid(gate) * conv(1)).astype(BF16)

    def down_stage(k):
        part = jnp.dot(s_ref[k % 3], wd_ref[k * FFN_COLS:(k + 1) * FFN_COLS, :], preferred_element_type=F32)
        if k == 0:
            y_ref[...] = x + part
        else:
            y_ref[...] += part

    up_stage(0)
    for k in range(n_stage):
        if k + 1 < n_stage:
            up_stage(k + 1)
        conv_stage(k)
        if k > 0:
            down_stage(k - 1)
    down_stage(n_stage - 1)

    y = y_ref[...]
    h = (_rms(y) * gp_ref[...]).astype(BF16)
    gate = jax.nn.sigmoid(jnp.dot(h, wgate_ref[...], preferred_element_type=F32))
    proj = jnp.dot(p_ref[...].astype(BF16), wproj_ref[...], preferred_element_type=F32)
    y_ref[...] = y + gate * proj


def _conv_ffn_ple(x, p, g_norm, w_up, conv_w, conv_b, w_down, g_ple, w_gate, w_proj, past, *, nb, tm, layer):
    n = x.shape[0]
    pd = p.shape[1]
    nt = n // nb // tm
    has_past = past is not None
    row = lambda i: (i, 0)
    p_row = lambda i: (layer * (n // tm) + i, 0)
    const = lambda i: (0, 0)
    in_specs = [
        pl.BlockSpec((tm, D_MODEL), row),
        pl.BlockSpec((1, D_MODEL), const),
        pl.BlockSpec((D_MODEL, 2 * D_FF), const),
        pl.BlockSpec((CONV_W, 2 * D_FF), const),
        pl.BlockSpec((1, 2 * D_FF), const),
        pl.BlockSpec((D_FF, D_MODEL), const),
        pl.BlockSpec((tm, pd), p_row),
        pl.BlockSpec((1, D_MODEL), const),
        pl.BlockSpec((D_MODEL, D_MODEL), const),
        pl.BlockSpec((pd, D_MODEL), const),
    ]
    args = [x, g_norm, w_up, conv_w, conv_b, w_down, p, g_ple, w_gate, w_proj]
    if has_past:
        in_specs.append(pl.BlockSpec((1, CONV_W - 1, 2 * D_FF), lambda i: (i // nt, 0, 0)))
        args.append(past)
    y, tails = pl.pallas_call(
        functools.partial(_ffn_body, tm=tm, nt=nt, has_past=has_past),
        out_shape=(jax.ShapeDtypeStruct((n, D_MODEL), F32),
                   jax.ShapeDtypeStruct((n // tm, CONV_W - 1, 2 * D_FF), F32)),
        grid=(n // tm,),
        in_specs=in_specs,
        out_specs=(pl.BlockSpec((tm, D_MODEL), row),
                   pl.BlockSpec((1, CONV_W - 1, 2 * D_FF), lambda i: (i, 0, 0))),
        scratch_shapes=[pltpu.VMEM((tm, D_MODEL), BF16),
                        pltpu.VMEM((2, 2, tm + HALO, FFN_COLS), F32),
                        pltpu.VMEM((3, tm, FFN_COLS), BF16),
                        pltpu.VMEM((HALO, 2 * D_FF), F32)],
        compiler_params=_cparams(("arbitrary",)),
        name="conv_ffn_ple",
    )(*args)
    return y, tails.reshape(nb, nt, CONV_W - 1, 2 * D_FF)[:, nt - 1]


def _cache_prep_body(ck_ref, cv_ref, cki_ref, kt_ref, v_ref, ki_ref, *, tm):
    for r in range(tm // LANES):
        rows = slice(r * LANES, (r + 1) * LANES)
        for g in range(N_KV_HEADS):
            cols = slice(g * HEAD_DIM, (g + 1) * HEAD_DIM)
            kt_ref[r, cols, :] = ck_ref[rows, g, :].T.astype(BF16)
            v_ref[rows, cols] = cv_ref[rows, g, :].astype(BF16)
        ki = cki_ref[rows, :]
        hi = ki.astype(BF16).astype(F32)
        ki_ref[rows, :] = jnp.concatenate([hi, hi, ki - hi, ki - hi], axis=1).astype(BF16)


def _cache_prep(ck, cv, cki, *, tm):
    n = ck.shape[0]
    kvw = N_KV_HEADS * HEAD_DIM
    kv_in = pl.BlockSpec((tm, N_KV_HEADS, HEAD_DIM), lambda i: (i, 0, 0))
    row = lambda i: (i, 0)
    return pl.pallas_call(
        functools.partial(_cache_prep_body, tm=tm),
        out_shape=(jax.ShapeDtypeStruct((n // LANES, kvw, LANES), BF16),
                   jax.ShapeDtypeStruct((n, kvw), BF16),
                   jax.ShapeDtypeStruct((n, 2 * LANES), BF16)),
        grid=(n // tm,),
        in_specs=[kv_in, kv_in, pl.BlockSpec((tm, IDX_DIM), row)],
        out_specs=(pl.BlockSpec((tm // LANES, kvw, LANES), lambda i: (i, 0, 0)),
                   pl.BlockSpec((tm, kvw), row), pl.BlockSpec((tm, 2 * LANES), row)),
        compiler_params=_cparams(("parallel",)),
        name="cache_prep",
    )(ck, cv, cki)


def _pad_rows(a, rows):
    return jnp.pad(a, ((0, 0), (0, rows - a.shape[1]), (0, 0)))


def _attn_layer(x, g_mix, w_in, q_g, k_g, w_out, cache, *, tm, tq):
    b, t, _ = x.shape
    n = b * t
    past_len = 0 if cache is None else cache[0].shape[1]
    pos = past_len + jnp.arange(t, dtype=I32)
    reps = tm // t if tm > t else 1
    tabs = tuple(jnp.tile(a, (reps, 1)) for a in _rope_tables(pos, HEAD_DIM) + _rope_tables(pos, IDX_DIM))
    n_tab = max(t // tm, 1)
    src = _proj_column_sources()
    w_wide = jnp.take(jnp.pad(w_in.astype(BF16), ((0, 0), (0, 1))), src, axis=1)
    k_gp = jnp.take(jnp.pad(k_g, (0, ATTN_PROJ + 1 - HEAD_DIM)), _partner_columns(0, HEAD_DIM))
    gains = (q_g[None, :], k_g[None, :], k_gp[None, :])
    q, kf, vf, kif, kt, vb, kic, qit, wt = _attn_proj(
        x.reshape(n, D_MODEL), g_mix[None, :], w_wide, gains, tabs, tm=tm, n_tab=n_tab)
    kvw = N_KV_HEADS * HEAD_DIM
    wo = w_out.astype(BF16)
    if cache is None:
        y = _sparse_attn(x, q.reshape(b, t, O_Q), qit, wt, kic.reshape(b, t, kvw), kt, vb.reshape(b, t, kvw), wo,
                         tq=tq, pos0=0, l_true=t, n_valid=tq)
    else:
        ck, cv, cki = cache
        l_true = past_len + t
        lp = -(-l_true // KEY_TILE) * KEY_TILE
        n_new = lp - past_len
        kt_c, v_c, ki_c = _cache_prep(ck.reshape(b * past_len, N_KV_HEADS, HEAD_DIM),
                                      cv.reshape(b * past_len, N_KV_HEADS, HEAD_DIM),
                                      cki.reshape(b * past_len, IDX_DIM), tm=512)
        k_new = _pad_rows(kf.reshape(b, t, kvw).astype(BF16), n_new)
        kt_new = k_new.reshape(b, n_new // LANES, LANES, kvw).transpose(0, 1, 3, 2)
        kt_all = jnp.concatenate([kt_c.reshape(b, past_len // LANES, kvw, LANES), kt_new], 1)
        kt_all = kt_all.reshape(b * lp // LANES, kvw, LANES)
        v_all = jnp.concatenate([v_c.reshape(b, past_len, kvw), _pad_rows(vb.reshape(b, t, kvw), n_new)], 1)
        ki_all = jnp.concatenate([ki_c.reshape(b, past_len, kvw), _pad_rows(kic.reshape(b, t, kvw), n_new)], 1)
        q_p = _pad_rows(q.reshape(b, t, O_Q), tq)
        qit_p = jnp.pad(qit.reshape(O_Q, b, t), ((0, 0), (0, 0), (0, tq - t))).transpose(1, 0, 2)
        wt_p = jnp.pad(wt.reshape(N_IDX_HEADS, b, t), ((0, 0), (0, 0), (0, tq - t))).reshape(N_IDX_HEADS, b * tq)
        y = _sparse_attn(_pad_rows(x, tq), q_p, qit_p, wt_p, ki_all, kt_all, v_all, wo,
                         tq=tq, pos0=past_len, l_true=l_true, n_valid=t)[:, :t]
    return (y, kf.reshape(b, t, N_KV_HEADS, HEAD_DIM),
            vf.reshape(b, t, N_KV_HEADS, HEAD_DIM), kif.reshape(b, t, IDX_DIM))


def _gmlp_layer(x, g_mix, w_in, v_g, w_s, b_s, w_out, *, tm, emit_v):
    b, t, _ = x.shape
    n = b * t
    tc = min(t, GM_CHUNK)
    reps = GM_CHUNK // tc
    ws = jnp.tile(w_s[:, :tc, :tc], (1, reps, reps))
    bs = jnp.broadcast_to(jnp.tile(b_s[:, :tc], (1, reps))[:, :, None], (GM_GROUPS, GM_CHUNK, GM_CHUNK))
    y, v = _gmlp(x.reshape(n, D_MODEL), g_mix[None, :], w_in.astype(BF16), v_g[None, :], ws, bs,
                 w_out.astype(BF16), tm=tm, tc=tc, emit_v=emit_v)
    return y.reshape(b, t, D_MODEL), (v.reshape(b, t, D_MODEL) if emit_v else None)


def _ffn_ple_layer(x, p_all, layer, past, g_ffn, w_up, conv_w, conv_b, w_down, g_ple, w_gate, w_proj, *, tm):
    b, t, _ = x.shape
    n = b * t
    y, state = _conv_ffn_ple(x.reshape(n, D_MODEL), p_all.reshape(-1, p_all.shape[-1]), g_ffn[None, :],
                             w_up.astype(BF16), conv_w, conv_b[None, :], w_down.astype(BF16),
                             g_ple[None, :], w_gate.astype(BF16), w_proj.astype(BF16), past,
                             nb=b, tm=tm, layer=layer)
    return y.reshape(b, t, D_MODEL), state


def kernel(x_prompt, x_sample, cache_k, cache_v, cache_kidx, state_ffn_conv, p_prompt, p_sample,
           norm_mix, attn_w_in, attn_q_norm, attn_k_norm, attn_w_out,
           gmlp_w_in, gmlp_v_norm, gmlp_w_spatial, gmlp_b_spatial, gmlp_w_out,
           norm_ffn, ffn_w_up, ffn_conv_w, ffn_conv_b, ffn_w_down,
           norm_ple, ple_w_gate, ple_w_proj):
    depth = norm_mix.shape[0]
    t_s = x_sample.shape[1]
    n_s = x_sample.shape[0] * t_s
    yp, ys = x_prompt, x_sample
    kp, vp, kip, ks, vs, kis, gvs, cps, css = [], [], [], [], [], [], [], [], []
    for i in range(depth):
        j = i // 2
        if i % 2 == 0:
            yp, k, v, ki = _attn_layer(yp, norm_mix[i], attn_w_in[j], attn_q_norm[j], attn_k_norm[j],
                                       attn_w_out[j], None, tm=512, tq=256)
            kp.append(k); vp.append(v); kip.append(ki)
            ys, k, v, ki = _attn_layer(ys, norm_mix[i], attn_w_in[j], attn_q_norm[j], attn_k_norm[j],
                                       attn_w_out[j], (cache_k[j], cache_v[j], cache_kidx[j]), tm=n_s, tq=128)
            ks.append(k); vs.append(v); kis.append(ki)
        else:
            yp, _ = _gmlp_layer(yp, norm_mix[i], gmlp_w_in[j], gmlp_v_norm[j], gmlp_w_spatial[j],
                                gmlp_b_spatial[j], gmlp_w_out[j], tm=512, emit_v=False)
            ys, gv = _gmlp_layer(ys, norm_mix[i], gmlp_w_in[j], gmlp_v_norm[j], gmlp_w_spatial[j],
                                 gmlp_b_spatial[j], gmlp_w_out[j], tm=n_s, emit_v=True)
            gvs.append(gv)
        ffn = (norm_ffn[i], ffn_w_up[i], ffn_conv_w[i], ffn_conv_b[i], ffn_w_down[i],
               norm_ple[i], ple_w_gate[i], ple_w_proj[i])
        yp, cp = _ffn_ple_layer(yp, p_prompt, i, None, *ffn, tm=512)
        ys, cs = _ffn_ple_layer(ys, p_sample, i, state_ffn_conv[i], *ffn, tm=t_s)
        cps.append(cp); css.append(cs)
    return (yp, ys, jnp.stack(kp, 0), jnp.stack(vp, 0), jnp.stack(kip, 0),
            jnp.stack(ks, 0), jnp.stack(vs, 0), jnp.stack(kis, 0), jnp.stack(gvs, 0),
            jnp.stack(cps, 0), jnp.stack(css, 0))
```

```python
import functools

import numpy as np
import jax
import jax.numpy as jnp
from jax import lax
from jax.experimental import pallas as pl
from jax.experimental.pallas import tpu as pltpu

F32 = jnp.float32
BF16 = jnp.bfloat16
I32 = jnp.int32
I16 = jnp.int16

D_MODEL = 1024
N_HEADS = 8
HEAD_DIM = 128
N_KV_HEADS = 2
N_IDX_HEADS = 8
IDX_DIM = 64
CHUNK = 64
TOPK_MAX = 256
ROPE_THETA = 500000.0
ROPE_FRAC_DIV = 4
GM_GROUPS = 8
GM_CHUNK = 128
D_FF = 2816
CONV_W = 3
EPS = 1e-6

O_Q = N_HEADS * HEAD_DIM
O_K = O_Q + N_KV_HEADS * HEAD_DIM
O_V = O_K + N_KV_HEADS * HEAD_DIM
O_QI = O_V + N_IDX_HEADS * IDX_DIM
O_KI = O_QI + IDX_DIM
ATTN_PROJ = O_KI + N_IDX_HEADS

LANES = 128
ATTN_PROJ_PAD = 2176
INT_MIN = -2147483648
I16_MIN = -32768
MASK_BIAS = -1e30
LOGIT_SCALE = (HEAD_DIM ** -0.5) * 1.4426950408889634
KEY_TILE = 256
COUNT_ROWS = 64
Q_ROWS = 128
ATTN_UNROLL = 4
VMEM_LIMIT = 56 * 1024 * 1024


def _cparams(sem, flags=None):
    return pltpu.CompilerParams(dimension_semantics=sem, vmem_limit_bytes=VMEM_LIMIT, flags=flags)


def _nt_dot(a, b):
    return lax.dot_general(a, b, (((1,), (1,)), ((), ())), preferred_element_type=F32)


def _rms(x):
    return x * lax.rsqrt(jnp.mean(x * x, axis=-1, keepdims=True) + EPS)


def _rope_tables(pos, head_w):
    rot = head_w // ROPE_FRAC_DIV
    half = rot // 2
    inv = ROPE_THETA ** (-jnp.arange(half, dtype=F32) * (2.0 / rot))
    ang = pos.astype(F32)[:, None] * inv[None, :]
    cos, sin = jnp.cos(ang), jnp.sin(ang)
    j = np.arange(LANES) % head_w
    idx = j % half
    first = jnp.asarray(j < half)[None, :]
    second = jnp.asarray((j >= half) & (j < rot))[None, :]
    cos_t = jnp.where(first | second, cos[:, idx], 1.0)
    sin_t = jnp.where(first, -sin[:, idx], jnp.where(second, sin[:, idx], 0.0))
    return cos_t.astype(F32), sin_t.astype(F32)


T_Q, T_K, T_V, T_QI, T_KI, T_W = 0, 8, 10, 12, 16, 17
T_KP, T_QIP, T_KIP = 18, 20, 24
PROJ_TILES = 25
PROJ_COLS = PROJ_TILES * LANES
ROPE_HALF = HEAD_DIM // ROPE_FRAC_DIV // 2


def _partner_columns(base, width):
    j = np.arange(LANES)
    half = width // ROPE_FRAC_DIV // 2
    jj = j % width
    out = np.full(LANES, ATTN_PROJ, np.int64)
    out[jj < half] = (base + j + half)[jj < half]
    sec = (jj >= half) & (jj < 2 * half)
    out[sec] = (base + j - half)[sec]
    return out


def _proj_column_sources():
    src = np.full((PROJ_TILES, LANES), ATTN_PROJ, np.int64)
    j = np.arange(LANES)
    for t in range(T_KI):
        src[t] = t * LANES + j
    src[T_KI] = O_QI + j % IDX_DIM
    src[T_W, :N_IDX_HEADS] = O_KI + j[:N_IDX_HEADS]
    for t in range(N_KV_HEADS):
        src[T_KP + t] = _partner_columns(O_Q + t * HEAD_DIM, HEAD_DIM)
    for t in range(T_KI - T_QI):
        src[T_QIP + t] = _partner_columns(O_V + t * LANES, IDX_DIM)
    kip = _partner_columns(0, IDX_DIM)
    src[T_KIP] = np.where(kip == ATTN_PROJ, ATTN_PROJ, O_QI + kip % IDX_DIM)
    return src.reshape(-1)


def _attn_proj_body(x_ref, gm_ref, w_ref, qg_ref, kg_ref, kgp_ref, cq_ref, sq_ref, ci_ref, si_ref,
                    q_ref, kf_ref, vf_ref, kif_ref, kt_ref, vb_ref, kic_ref, qit_ref, wt_ref,
                    z_ref, *, tm):
    x = x_ref[...]
    h = _rms(x) * gm_ref[...]
    z_ref[...] = jnp.dot(h.astype(BF16), w_ref[...], preferred_element_type=F32)

    rb = LANES
    w_scale = (N_IDX_HEADS ** -0.5) * (IDX_DIM ** -0.5)
    first_half = lax.broadcasted_iota(I32, (rb, LANES), 1) < ROPE_HALF

    def tile(rows, t):
        return z_ref[rows, t * LANES:(t + 1) * LANES]

    def normed_rope(rows, t, tp, g_ref, gp_ref, c, s):
        z = tile(rows, t)
        rs = lax.rsqrt(jnp.mean(z * z, axis=-1, keepdims=True) + EPS)
        y = z * rs * g_ref[...]
        if tp is None:
            yp = jnp.where(first_half, pltpu.roll(y, LANES - ROPE_HALF, 1), pltpu.roll(y, ROPE_HALF, 1))
        else:
            yp = tile(rows, tp) * rs * gp_ref[...]
        return y * c + yp * s

    def split(y):
        hi = y.astype(BF16).astype(F32)
        return hi, y - hi

    for r in range(tm // rb):
        rows = slice(r * rb, (r + 1) * rb)
        cq, sq = cq_ref[rows, :], sq_ref[rows, :]
        ci, si = ci_ref[rows, :], si_ref[rows, :]
        for hd in range(N_HEADS):
            cols = slice(hd * HEAD_DIM, (hd + 1) * HEAD_DIM)
            y = normed_rope(rows, T_Q + hd, None, qg_ref, None, cq, sq)
            q_ref[rows, cols] = (y * LOGIT_SCALE).astype(BF16)
        for g in range(N_KV_HEADS):
            cols = slice(g * HEAD_DIM, (g + 1) * HEAD_DIM)
            y = normed_rope(rows, T_K + g, T_KP + g, kg_ref, kgp_ref, cq, sq)
            kf_ref[rows, g, :] = y
            kt_ref[r, cols, :] = y.T.astype(BF16)
            v = tile(rows, T_V + g)
            vf_ref[rows, g, :] = v
            vb_ref[rows, cols] = v.astype(BF16)
        for t in range(T_KI - T_QI):
            y = tile(rows, T_QI + t) * ci + tile(rows, T_QIP + t) * si
            hi, lo = split(y)
            hi_t, lo_t = hi.T.astype(BF16), lo.T.astype(BF16)
            for u in range(2):
                base = (2 * t + u) * LANES
                dims = slice(u * IDX_DIM, (u + 1) * IDX_DIM)
                qit_ref[r, base:base + IDX_DIM, :] = hi_t[dims, :]
                qit_ref[r, base + IDX_DIM:base + LANES, :] = lo_t[dims, :]
        y = tile(rows, T_KI) * ci + tile(rows, T_KIP) * si
        kif_ref[rows, :] = y[:, :IDX_DIM]
        hi, lo = split(y)
        kic_ref[rows, 0:LANES] = hi.astype(BF16)
        kic_ref[rows, LANES:2 * LANES] = lo.astype(BF16)
        wt_ref[:, rows] = tile(rows, T_W).T[:N_IDX_HEADS, :] * w_scale


def _attn_proj(x, g_mix, w_in, gains, tabs, *, tm, n_tab):
    n = x.shape[0]
    cq, sq, ci, si = tabs
    row = lambda i: (i, 0)
    tab = lambda i: (i % n_tab, 0)
    const = lambda i: (0, 0)
    out_shape = (
        jax.ShapeDtypeStruct((n, O_Q), BF16),
        jax.ShapeDtypeStruct((n, N_KV_HEADS, HEAD_DIM), F32),
        jax.ShapeDtypeStruct((n, N_KV_HEADS, HEAD_DIM), F32),
        jax.ShapeDtypeStruct((n, IDX_DIM), F32),
        jax.ShapeDtypeStruct((n // LANES, N_KV_HEADS * HEAD_DIM, LANES), BF16),
        jax.ShapeDtypeStruct((n, N_KV_HEADS * HEAD_DIM), BF16),
        jax.ShapeDtypeStruct((n, 2 * LANES), BF16),
        jax.ShapeDtypeStruct((n // LANES, N_IDX_HEADS * LANES, LANES), BF16),
        jax.ShapeDtypeStruct((N_IDX_HEADS, n), F32),
    )
    tile3 = lambda i: (i, 0, 0)
    rspec = lambda w: pl.BlockSpec((tm, w), row)
    kvspec = pl.BlockSpec((tm, N_KV_HEADS, HEAD_DIM), tile3)
    out_specs = (rspec(O_Q), kvspec, kvspec, rspec(IDX_DIM),
                 pl.BlockSpec((tm // LANES, N_KV_HEADS * HEAD_DIM, LANES), tile3),
                 rspec(256), rspec(2 * LANES),
                 pl.BlockSpec((tm // LANES, N_IDX_HEADS * LANES, LANES), tile3),
                 pl.BlockSpec((N_IDX_HEADS, tm), lambda i: (0, i)))
    return pl.pallas_call(
        functools.partial(_attn_proj_body, tm=tm),
        out_shape=out_shape,
        grid=(n // tm,),
        in_specs=[
            pl.BlockSpec((tm, D_MODEL), row),
            pl.BlockSpec((1, D_MODEL), const),
            pl.BlockSpec((D_MODEL, PROJ_COLS), const),
            pl.BlockSpec((1, HEAD_DIM), const),
            pl.BlockSpec((1, HEAD_DIM), const),
            pl.BlockSpec((1, HEAD_DIM), const),
            pl.BlockSpec((tm, LANES), tab),
            pl.BlockSpec((tm, LANES), tab),
            pl.BlockSpec((tm, LANES), tab),
            pl.BlockSpec((tm, LANES), tab),
        ],
        out_specs=out_specs,
        scratch_shapes=[pltpu.VMEM((tm, PROJ_COLS), F32)],
        compiler_params=_cparams(("parallel",)),
        name="attn_proj",
    )(x, g_mix, w_in, *gains, cq, sq, ci, si)


def _sparse_attn_body(x_ref, q_ref, qit_ref, wt_ref, ki_ref, kt_ref, v_ref, wo_ref, y_ref,
                      keys_ref, hi_ref, lo_ref, bias_ref, sacc_ref, qs_ref, s_ref, p_ref, alpha_ref, m_ref,
                      acc_ref, o_ref, *, tq, pos0, l_true, n_valid, k_sel):
    j = pl.program_id(1)
    q0 = pos0 + j * tq
    nqt = tq // LANES
    lane_q = lax.broadcasted_iota(I32, (1, tq), 1)
    qpos = q0 + lane_q
    limit = jnp.minimum((lax.shift_right_logical(qpos, 6) + 1) * CHUNK, l_true)
    lim_max = jnp.minimum((lax.shift_right_logical(q0 + tq - 1, 6) + 1) * CHUNK, l_true)
    n_kt = lax.shift_right_logical(lim_max + KEY_TILE - 1, 8)

    def score_rows(r0, nrows, slot):
        rows = pl.ds(r0, nrows)
        kt = ki_ref[0, rows, :]
        for h in range(N_IDX_HEADS):
            qh = jnp.concatenate([qit_ref[u, h * LANES:(h + 1) * LANES, :] for u in range(nqt)], axis=1)
            s = jnp.dot(kt, jnp.concatenate([qh, qh], axis=0), preferred_element_type=F32)
            term = wt_ref[h:h + 1, :] * jnp.maximum(s, 0.0)
            if h == 0:
                sacc_ref[slot, 0:nrows, :] = term
            elif h < N_IDX_HEADS - 1:
                sacc_ref[slot, 0:nrows, :] += term
            else:
                bits = pltpu.bitcast(sacc_ref[slot, 0:nrows, :] + term, I32)
                key = jnp.where(bits < 0, INT_MIN - bits, bits)
                key = jnp.where(r0 + lax.broadcasted_iota(I32, (nrows, tq), 0) < limit, key, INT_MIN)
                keys_ref[rows, :] = key
                hi_ref[rows, :] = lax.shift_right_arithmetic(key, 16).astype(I16)
                lo_ref[rows, :] = ((key & 0xFFFF) + I16_MIN).astype(I16)

    def score_quad(i, carry):
        r0 = pl.multiple_of(i * 4 * KEY_TILE, 4 * KEY_TILE)
        score_rows(r0, 2 * KEY_TILE, 0)
        score_rows(r0 + 2 * KEY_TILE, 2 * KEY_TILE, 1)
        return carry

    def score_pair(i, carry):
        score_rows(pl.multiple_of((n_kt & -4) * KEY_TILE, 2 * KEY_TILE), 2 * KEY_TILE, 0)
        return carry

    def score_last(i, carry):
        score_rows(pl.multiple_of((n_kt - 1) * KEY_TILE, KEY_TILE), KEY_TILE, 0)
        return carry

    lax.fori_loop(0, lax.shift_right_logical(n_kt, 2), score_quad, 0)
    lax.fori_loop(0, lax.shift_right_logical(n_kt, 1) & 1, score_pair, 0)
    lax.fori_loop(0, n_kt & 1, score_last, 0)

    def count(pred):
        def body(t, acc):
            for u in range(KEY_TILE // COUNT_ROWS):
                r0 = pl.multiple_of(t * KEY_TILE + u * COUNT_ROWS, COUNT_ROWS)
                acc = acc + jnp.where(pred(keys_ref[pl.ds(r0, COUNT_ROWS), :], r0), 1.0, 0.0)
            return acc
        acc = lax.fori_loop(0, n_kt, body, jnp.zeros((COUNT_ROWS, tq), F32))
        return jnp.sum(acc, axis=0, keepdims=True)

    def count16(ref, cand):
        c16 = jnp.broadcast_to(cand, (COUNT_ROWS, tq)).astype(I16)

        def body(t, acc):
            for u in range(KEY_TILE // COUNT_ROWS):
                r0 = pl.multiple_of(t * KEY_TILE + u * COUNT_ROWS, COUNT_ROWS)
                acc = acc + jnp.where(ref[pl.ds(r0, COUNT_ROWS), :] >= c16, jnp.int16(1), jnp.int16(0))
            return acc
        acc = lax.fori_loop(0, n_kt, body, jnp.zeros((COUNT_ROWS, tq), I16))
        return jnp.sum(acc.astype(I32).astype(F32), axis=0, keepdims=True)

    def search16(ref, need):
        def step(p, ans):
            cand = ans + lax.shift_left(jnp.int32(1), 15 - p)
            return jnp.where(count16(ref, cand) >= need, cand, ans)
        return lax.fori_loop(0, 16, step, jnp.full((1, tq), I16_MIN, I32))

    kf = float(k_sel)
    hi_k = search16(hi_ref, kf)
    n_above = jnp.where(hi_k >= -I16_MIN - 1, 0.0, count16(hi_ref, hi_k + 1))
    hi_k16 = jnp.broadcast_to(hi_k, (KEY_TILE, tq)).astype(I16)

    def keep_lo(t, carry):
        r0 = pl.multiple_of(t * KEY_TILE, KEY_TILE)
        rows = pl.ds(r0, KEY_TILE)
        lo_ref[rows, :] = jnp.where(hi_ref[rows, :] == hi_k16, lo_ref[rows, :], jnp.int16(I16_MIN))
        return carry

    lax.fori_loop(0, n_kt, keep_lo, 0)
    lo_k = search16(lo_ref, kf - n_above)
    thr = hi_k * 65536 + (lo_k - I16_MIN)
    thr_c = jnp.maximum(thr, INT_MIN + 1)
    n_ge = count(lambda c, r0: c >= thr_c)

    def to_rows(v):
        b = jnp.broadcast_to(v, (LANES, tq))
        col = jnp.concatenate([b[:, u * LANES:(u + 1) * LANES].T for u in range(nqt)], axis=0)
        return jnp.concatenate([col] * (KEY_TILE // LANES), axis=1)

    def write_bias(selected):
        def tile(t):
            r0 = pl.multiple_of(t * KEY_TILE, KEY_TILE)
            kq = keys_ref[pl.ds(r0, KEY_TILE), :].T
            bias_ref[t] = jnp.where(selected(kq, r0), 0.0, MASK_BIAS)

        def pair(i, carry):
            tile(2 * i)
            tile(2 * i + 1)
            return carry

        def last(i, carry):
            tile(n_kt - 1)
            return carry

        lax.fori_loop(0, lax.shift_right_logical(n_kt, 1), pair, 0)
        lax.fori_loop(0, n_kt & 1, last, 0)

    thr_rows = to_rows(thr_c)
    write_bias(lambda kq, r0: kq >= thr_rows)

    excess = jnp.where((n_ge > kf) & (lane_q < n_valid), 1.0, 0.0)

    @pl.when(jnp.max(excess) > 0.0)
    def _():
        need = kf - count(lambda c, r0: c > thr)
        idx_iota = lax.broadcasted_iota(I32, (COUNT_ROWS, tq), 0)
        last = jnp.zeros((1, tq), I32)
        for bit in range(13, -1, -1):
            cand = last | (1 << bit)
            cnt = count(lambda c, r0: (c == thr) & (r0 + idx_iota < cand))
            last = jnp.where(cnt < need, cand, last)
        key_pos = lax.broadcasted_iota(I32, (tq, KEY_TILE), 1)
        t_rows, l_rows = to_rows(thr), to_rows(last)
        write_bias(lambda kq, r0: (kq > t_rows) | ((kq == t_rows) & (kq > INT_MIN) & (r0 + key_pos <= l_rows)))

    hpg = N_HEADS // N_KV_HEADS
    for g in range(N_KV_HEADS):
        qs_ref[g] = jnp.concatenate(
            [q_ref[0, :, (g * hpg + hh) * HEAD_DIM:(g * hpg + hh + 1) * HEAD_DIM] for hh in range(hpg)], axis=0)
    m_ref[...] = jnp.full(m_ref.shape, MASK_BIAS, F32)
    acc_ref[...] = jnp.zeros(acc_ref.shape, F32)
    ones = jnp.ones((KEY_TILE, HEAD_DIM), BF16)

    def attn_tile(t, slot):
        r0 = pl.multiple_of(t * KEY_TILE, KEY_TILE)
        for g in range(N_KV_HEADS):
            gcols = slice(g * HEAD_DIM, (g + 1) * HEAD_DIM)
            kt = jnp.concatenate([kt_ref[2 * t + u, gcols, :] for u in range(KEY_TILE // LANES)], axis=1)
            s_ref[slot] = jnp.dot(qs_ref[g], kt, preferred_element_type=F32)
            for rb in range(hpg * nqt):
                rows = slice(rb * Q_ROWS, (rb + 1) * Q_ROWS)
                qrows = slice((rb % nqt) * Q_ROWS, (rb % nqt + 1) * Q_ROWS)
                s = s_ref[slot, rows, :] + bias_ref[t, qrows, :]
                s0, s1 = s[:, :LANES], s[:, LANES:]
                m_old = m_ref[g, rows, :]
                m_new = jnp.maximum(m_old, jnp.max(jnp.maximum(s0, s1), axis=-1, keepdims=True))
                alpha_ref[slot, rows, :] = jnp.exp2(m_old - m_new)
                m_ref[g, rows, :] = m_new
                p_ref[slot, rows, :LANES] = jnp.exp2(s0 - m_new).astype(BF16)
                p_ref[slot, rows, LANES:] = jnp.exp2(s1 - m_new).astype(BF16)
            vt = jnp.concatenate([v_ref[0, pl.ds(r0, KEY_TILE), gcols], ones], axis=1)
            a = alpha_ref[slot]
            acc_ref[g] = acc_ref[g] * jnp.concatenate([a, a], axis=1) + jnp.dot(
                p_ref[slot], vt, preferred_element_type=F32)

    def attn_group(first_tile, width):
        def body(i, carry):
            for u in range(width):
                attn_tile(first_tile + i * width + u, u)
            return carry
        return body

    n4 = lax.shift_right_logical(n_kt, 2)
    n2 = lax.shift_right_logical(n_kt, 1) & 1
    lax.fori_loop(0, n4, attn_group(0, ATTN_UNROLL), 0)
    lax.fori_loop(0, n2, attn_group(n4 * 4, 2), 0)
    lax.fori_loop(0, n_kt & 1, attn_group(n_kt - 1, 1), 0)
    for h in range(N_HEADS):
        g, hh = divmod(h, hpg)
        rows = slice(hh * tq, (hh + 1) * tq)
        o_ref[:, h * HEAD_DIM:(h + 1) * HEAD_DIM] = (
            acc_ref[g, rows, :HEAD_DIM] / acc_ref[g, rows, HEAD_DIM:]).astype(BF16)
    y_ref[0] = x_ref[0] + jnp.dot(o_ref[...], wo_ref[...], preferred_element_type=F32)


def _sparse_attn(x, q, qit, wt, ki, kt, v, w_out, *, tq, pos0, l_true, n_valid):
    b, t_q, _ = q.shape
    lp = v.shape[1]
    nq = t_q // tq
    nqt = tq // LANES
    k_sel = min(TOPK_MAX, l_true // 4)
    hpg = N_HEADS // N_KV_HEADS
    ar = hpg * tq
    body = functools.partial(_sparse_attn_body, tq=tq, pos0=pos0, l_true=l_true,
                             n_valid=n_valid, k_sel=k_sel)
    xspec = pl.BlockSpec((1, tq, D_MODEL), lambda i, j: (i, j, 0))
    qspec = pl.BlockSpec((1, tq, O_Q), lambda i, j: (i, j, 0))
    kspec = pl.BlockSpec((1, lp, 2 * LANES), lambda i, j: (i, 0, 0))
    return pl.pallas_call(
        body,
        out_shape=jax.ShapeDtypeStruct((b, t_q, D_MODEL), F32),
        grid=(b, nq),
        in_specs=[xspec, qspec,
                  pl.BlockSpec((nqt, N_IDX_HEADS * LANES, LANES), lambda i, j: (i * nq + j, 0, 0)),
                  pl.BlockSpec((N_IDX_HEADS, tq), lambda i, j: (0, i * nq + j)),
                  kspec,
                  pl.BlockSpec((lp // LANES, 2 * LANES, LANES), lambda i, j: (i, 0, 0)),
                  kspec,
                  pl.BlockSpec((O_Q, D_MODEL), lambda i, j: (0, 0))],
        out_specs=xspec,
        scratch_shapes=[pltpu.VMEM((lp, tq), I32), pltpu.VMEM((lp, tq), I16), pltpu.VMEM((lp, tq), I16),
                        pltpu.VMEM((lp // KEY_TILE, tq, KEY_TILE), F32),
                        pltpu.VMEM((2, 2 * KEY_TILE, tq), F32),
                        pltpu.VMEM((N_KV_HEADS, ar, HEAD_DIM), BF16),
                        pltpu.VMEM((ATTN_UNROLL, ar, KEY_TILE), F32),
                        pltpu.VMEM((ATTN_UNROLL, ar, KEY_TILE), BF16),
                        pltpu.VMEM((ATTN_UNROLL, ar, LANES), F32),
                        pltpu.VMEM((N_KV_HEADS, ar, LANES), F32),
                        pltpu.VMEM((N_KV_HEADS, ar, 2 * HEAD_DIM), F32),
                        pltpu.VMEM((tq, O_Q), BF16)],
        compiler_params=_cparams(("parallel", "arbitrary")),
        name="sparse_attn",
    )(x, q, qit, wt, ki, kt, v, w_out)


def _gmlp_body(x_ref, gm_ref, w_in_ref, vg_ref, ws_ref, bs_ref, w_out_ref, *rest, tm, tc, emit_v):
    if emit_v:
        y_ref, v_ref, z_ref, s_ref = rest
    else:
        y_ref, z_ref, s_ref = rest
    x = x_ref[...]
    h = _rms(x) * gm_ref[...]
    hb = h.astype(BF16)
    z_ref[:, D_MODEL:] = jax.nn.gelu(jnp.dot(hb, w_in_ref[:, D_MODEL:], preferred_element_type=F32))
    z_ref[:, :D_MODEL] = jax.nn.gelu(jnp.dot(hb, w_in_ref[:, :D_MODEL], preferred_element_type=F32))
    v = _rms(z_ref[:, D_MODEL:]) * vg_ref[...]
    if emit_v:
        v_ref[...] = v
    z_ref[:, D_MODEL:] = v
    gw = D_MODEL // GM_GROUPS
    r_i = lax.broadcasted_iota(I32, (GM_CHUNK, GM_CHUNK), 0)
    c_i = lax.broadcasted_iota(I32, (GM_CHUNK, GM_CHUNK), 1)
    keep = (r_i // tc == c_i // tc) & (c_i <= r_i)
    for g in range(GM_GROUPS):
        ws = jnp.where(keep, ws_ref[g], 0.0).astype(BF16)
        bias = bs_ref[g]
        cols = slice(g * gw, (g + 1) * gw)
        for ch in range(tm // GM_CHUNK):
            rows = slice(ch * GM_CHUNK, (ch + 1) * GM_CHUNK)
            vb = z_ref[rows, D_MODEL + g * gw:D_MODEL + (g + 1) * gw].astype(BF16)
            mixed = jnp.dot(ws, vb, preferred_element_type=F32) + bias
            s_ref[rows, cols] = (z_ref[rows, cols] * mixed).astype(BF16)
    y_ref[...] = x + jnp.dot(s_ref[...], w_out_ref[...], preferred_element_type=F32)


def _gmlp(x, g_mix, w_in, v_g, ws, bs, w_out, *, tm, tc, emit_v):
    n = x.shape[0]
    row = lambda i: (i, 0)
    const2 = lambda i: (0, 0)
    out_shape = [jax.ShapeDtypeStruct((n, D_MODEL), F32)]
    out_specs = [pl.BlockSpec((tm, D_MODEL), row)]
    if emit_v:
        out_shape.append(jax.ShapeDtypeStruct((n, D_MODEL), F32))
        out_specs.append(pl.BlockSpec((tm, D_MODEL), row))
    res = pl.pallas_call(
        functools.partial(_gmlp_body, tm=tm, tc=tc, emit_v=emit_v),
        out_shape=tuple(out_shape),
        grid=(n // tm,),
        in_specs=[pl.BlockSpec((tm, D_MODEL), row), pl.BlockSpec((1, D_MODEL), const2),
                  pl.BlockSpec((D_MODEL, 2 * D_MODEL), const2), pl.BlockSpec((1, D_MODEL), const2),
                  pl.BlockSpec((GM_GROUPS, GM_CHUNK, GM_CHUNK), lambda i: (0, 0, 0)),
                  pl.BlockSpec((GM_GROUPS, GM_CHUNK, GM_CHUNK), lambda i: (0, 0, 0)),
                  pl.BlockSpec((D_MODEL, D_MODEL), const2)],
        out_specs=tuple(out_specs),
        scratch_shapes=[pltpu.VMEM((tm, 2 * D_MODEL), F32), pltpu.VMEM((tm, D_MODEL), BF16)],
        compiler_params=_cparams(("parallel",)),
        name="gmlp",
    )(x, g_mix, w_in, v_g, ws, bs, w_out)
    return res if emit_v else (res[0], None)


HALO = 8


FFN_COLS = 256
FFN_ROWS = 128


def _ffn_body(*refs, tm, nt, has_past):
    if has_past:
        (x_ref, gn_ref, wup_ref, cw_ref, cb_ref, wd_ref, p_ref, gp_ref, wgate_ref, wproj_ref, past_ref,
         y_ref, st_ref, hb_ref, stage_ref, s_ref, car_ref) = refs
    else:
        (x_ref, gn_ref, wup_ref, cw_ref, cb_ref, wd_ref, p_ref, gp_ref, wgate_ref, wproj_ref,
         y_ref, st_ref, hb_ref, stage_ref, s_ref, car_ref) = refs
        past_ref = None
    first = lax.rem(pl.program_id(0), nt) == 0

    @pl.when(pl.program_id(0) == 0)
    def _():
        car_ref[...] = jnp.zeros(car_ref.shape, F32)

    x = x_ref[...]
    hb_ref[...] = (_rms(x) * gn_ref[...]).astype(BF16)
    rb = min(tm, FFN_ROWS)
    n_stage = D_FF // FFN_COLS

    def cols_of(k, br):
        return slice(br * D_FF + k * FFN_COLS, br * D_FF + (k + 1) * FFN_COLS)

    def up_stage(k):
        for br in range(2):
            cols = cols_of(k, br)
            if past_ref is None:
                init = jnp.zeros((CONV_W - 1, FFN_COLS), F32)
            else:
                init = past_ref[0, :, cols]
            stage_ref[k % 2, br, HALO - 2:HALO, :] = jnp.where(first, init, car_ref[0:CONV_W - 1, cols])
            stage_ref[k % 2, br, HALO:HALO + tm, :] = jnp.dot(
                hb_ref[...], wup_ref[:, cols], preferred_element_type=F32)

    def conv_stage(k):
        slot = k % 2
        for br in range(2):
            tail = stage_ref[slot, br, HALO + tm - 2:HALO + tm, :]
            car_ref[0:CONV_W - 1, cols_of(k, br)] = tail
            st_ref[0, :, cols_of(k, br)] = tail
        for r in range(tm // rb):
            for cc in range(FFN_COLS // LANES):
                lanes = slice(cc * LANES, (cc + 1) * LANES)

                def conv(br):
                    c0 = br * D_FF + k * FFN_COLS + cc * LANES
                    wcol = slice(c0, c0 + LANES)
                    r0 = HALO + r * rb
                    return (cb_ref[:, wcol]
                            + cw_ref[2:3, wcol] * stage_ref[slot, br, r0:r0 + rb, lanes]
                            + cw_ref[1:2, wcol] * stage_ref[slot, br, r0 - 1:r0 - 1 + rb, lanes]
                            + cw_ref[0:1, wcol] * stage_ref[slot, br, r0 - 2:r0 - 2 + rb, lanes])

                gate = conv(0)
                s_ref[k % 3, r * rb:(r + 1) * rb, lanes] = (gate * jax.nn.sigmoid(gate) * conv(1)).astype(BF16)

    def down_stage(k):
        part = jnp.dot(s_ref[k % 3], wd_ref[k * FFN_COLS:(k + 1) * FFN_COLS, :], preferred_element_type=F32)
        if k == 0:
            y_ref[...] = x + part
        else:
            y_ref[...] += part

    up_stage(0)
    for k in range(n_stage):
        if k + 1 < n_stage:
            up_stage(k + 1)
        conv_stage(k)
        if k > 0:
            down_stage(k - 1)
    down_stage(n_stage - 1)

    y = y_ref[...]
    h = (_rms(y) * gp_ref[...]).astype(BF16)
    gate = jax.nn.sigmoid(jnp.dot(h, wgate_ref[...], preferred_element_type=F32))
    proj = jnp.dot(p_ref[...].astype(BF16), wproj_ref[...], preferred_element_type=F32)
    y_ref[...] = y + gate * proj


def _conv_ffn_ple(x, p, g_norm, w_up, conv_w, conv_b, w_down, g_ple, w_gate, w_proj, past, *, nb, tm, layer):
    n = x.shape[0]
    pd = p.shape[1]
    nt = n // nb // tm
    has_past = past is not None
    row = lambda i: (i, 0)
    p_row = lambda i: (layer * (n // tm) + i, 0)
    const = lambda i: (0, 0)
    in_specs = [
        pl.BlockSpec((tm, D_MODEL), row),
        pl.BlockSpec((1, D_MODEL), const),
        pl.BlockSpec((D_MODEL, 2 * D_FF), const),
        pl.BlockSpec((CONV_W, 2 * D_FF), const),
        pl.BlockSpec((1, 2 * D_FF), const),
        pl.BlockSpec((D_FF, D_MODEL), const),
        pl.BlockSpec((tm, pd), p_row),
        pl.BlockSpec((1, D_MODEL), const),
        pl.BlockSpec((D_MODEL, D_MODEL), const),
        pl.BlockSpec((pd, D_MODEL), const),
    ]
    args = [x, g_norm, w_up, conv_w, conv_b, w_down, p, g_ple, w_gate, w_proj]
    if has_past:
        in_specs.append(pl.BlockSpec((1, CONV_W - 1, 2 * D_FF), lambda i: (i // nt, 0, 0)))
        args.append(past)
    y, tails = pl.pallas_call(
        functools.partial(_ffn_body, tm=tm, nt=nt, has_past=has_past),
        out_shape=(jax.ShapeDtypeStruct((n, D_MODEL), F32),
                   jax.ShapeDtypeStruct((n // tm, CONV_W - 1, 2 * D_FF), F32)),
        grid=(n // tm,),
        in_specs=in_specs,
        out_specs=(pl.BlockSpec((tm, D_MODEL), row),
                   pl.BlockSpec((1, CONV_W - 1, 2 * D_FF), lambda i: (i, 0, 0))),
        scratch_shapes=[pltpu.VMEM((tm, D_MODEL), BF16),
                        pltpu.VMEM((2, 2, tm + HALO, FFN_COLS), F32),
                        pltpu.VMEM((3, tm, FFN_COLS), BF16),
                        pltpu.VMEM((HALO, 2 * D_FF), F32)],
        compiler_params=_cparams(("arbitrary",)),
        name="conv_ffn_ple",
    )(*args)
    return y, tails.reshape(nb, nt, CONV_W - 1, 2 * D_FF)[:, nt - 1]


def _cache_prep_body(ck_ref, cv_ref, cki_ref, kt_ref, v_ref, ki_ref, *, tm):
    for r in range(tm // LANES):
        rows = slice(r * LANES, (r + 1) * LANES)
        for g in range(N_KV_HEADS):
            cols = slice(g * HEAD_DIM, (g + 1) * HEAD_DIM)
            kt_ref[r, cols, :] = ck_ref[rows, g, :].T.astype(BF16)
            v_ref[rows, cols] = cv_ref[rows, g, :].astype(BF16)
        ki = cki_ref[rows, :]
        hi = ki.astype(BF16).astype(F32)
        ki_ref[rows, :] = jnp.concatenate([hi, hi, ki - hi, ki - hi], axis=1).astype(BF16)


def _cache_prep(ck, cv, cki, *, tm):
    n = ck.shape[0]
    kvw = N_KV_HEADS * HEAD_DIM
    kv_in = pl.BlockSpec((tm, N_KV_HEADS, HEAD_DIM), lambda i: (i, 0, 0))
    row = lambda i: (i, 0)
    return pl.pallas_call(
        functools.partial(_cache_prep_body, tm=tm),
        out_shape=(jax.ShapeDtypeStruct((n // LANES, kvw, LANES), BF16),
                   jax.ShapeDtypeStruct((n, kvw), BF16),
                   jax.ShapeDtypeStruct((n, 2 * LANES), BF16)),
        grid=(n // tm,),
        in_specs=[kv_in, kv_in, pl.BlockSpec((tm, IDX_DIM), row)],
        out_specs=(pl.BlockSpec((tm // LANES, kvw, LANES), lambda i: (i, 0, 0)),
                   pl.BlockSpec((tm, kvw), row), pl.BlockSpec((tm, 2 * LANES), row)),
        compiler_params=_cparams(("parallel",)),
        name="cache_prep",
    )(ck, cv, cki)


def _pad_rows(a, rows):
    return jnp.pad(a, ((0, 0), (0, rows - a.shape[1]), (0, 0)))


def _attn_layer(x, g_mix, w_in, q_g, k_g, w_out, cache, *, tm, tq):
    b, t, _ = x.shape
    n = b * t
    past_len = 0 if cache is None else cache[0].shape[1]
    pos = past_len + jnp.arange(t, dtype=I32)
    reps = tm // t if tm > t else 1
    tabs = tuple(jnp.tile(a, (reps, 1)) for a in _rope_tables(pos, HEAD_DIM) + _rope_tables(pos, IDX_DIM))
    n_tab = max(t // tm, 1)
    src = _proj_column_sources()
    w_wide = jnp.take(jnp.pad(w_in.astype(BF16), ((0, 0), (0, 1))), src, axis=1)
    k_gp = jnp.take(jnp.pad(k_g, (0, ATTN_PROJ + 1 - HEAD_DIM)), _partner_columns(0, HEAD_DIM))
    gains = (q_g[None, :], k_g[None, :], k_gp[None, :])
    q, kf, vf, kif, kt, vb, kic, qit, wt = _attn_proj(
        x.reshape(n, D_MODEL), g_mix[None, :], w_wide, gains, tabs, tm=tm, n_tab=n_tab)
    kvw = N_KV_HEADS * HEAD_DIM
    wo = w_out.astype(BF16)
    if cache is None:
        y = _sparse_attn(x, q.reshape(b, t, O_Q), qit, wt, kic.reshape(b, t, kvw), kt, vb.reshape(b, t, kvw), wo,
                         tq=tq, pos0=0, l_true=t, n_valid=tq)
    else:
        ck, cv, cki = cache
        l_true = past_len + t
        lp = -(-l_true // KEY_TILE) * KEY_TILE
        n_new = lp - past_len
        kt_c, v_c, ki_c = _cache_prep(ck.reshape(b * past_len, N_KV_HEADS, HEAD_DIM),
                                      cv.reshape(b * past_len, N_KV_HEADS, HEAD_DIM),
                                      cki.reshape(b * past_len, IDX_DIM), tm=512)
        k_new = _pad_rows(kf.reshape(b, t, kvw).astype(BF16), n_new)
        kt_new = k_new.reshape(b, n_new // LANES, LANES, kvw).transpose(0, 1, 3, 2)
        kt_all = jnp.concatenate([kt_c.reshape(b, past_len // LANES, kvw, LANES), kt_new], 1)
        kt_all = kt_all.reshape(b * lp // LANES, kvw, LANES)
        v_all = jnp.concatenate([v_c.reshape(b, past_len, kvw), _pad_rows(vb.reshape(b, t, kvw), n_new)], 1)
        ki_all = jnp.concatenate([ki_c.reshape(b, past_len, kvw), _pad_rows(kic.reshape(b, t, kvw), n_new)], 1)
        q_p = _pad_rows(q.reshape(b, t, O_Q), tq)
        qit_p = jnp.pad(qit.reshape(O_Q, b, t), ((0, 0), (0, 0), (0, tq - t))).transpose(1, 0, 2)
        wt_p = jnp.pad(wt.reshape(N_IDX_HEADS, b, t), ((0, 0), (0, 0), (0, tq - t))).reshape(N_IDX_HEADS, b * tq)
        y = _sparse_attn(_pad_rows(x, tq), q_p, qit_p, wt_p, ki_all, kt_all, v_all, wo,
                         tq=tq, pos0=past_len, l_true=l_true, n_valid=t)[:, :t]
    return (y, kf.reshape(b, t, N_KV_HEADS, HEAD_DIM),
            vf.reshape(b, t, N_KV_HEADS, HEAD_DIM), kif.reshape(b, t, IDX_DIM))


def _gmlp_layer(x, g_mix, w_in, v_g, w_s, b_s, w_out, *, tm, emit_v):
    b, t, _ = x.shape
    n = b * t
    tc = min(t, GM_CHUNK)
    reps = GM_CHUNK // tc
    ws = jnp.tile(w_s[:, :tc, :tc], (1, reps, reps))
    bs = jnp.broadcast_to(jnp.tile(b_s[:, :tc], (1, reps))[:, :, None], (GM_GROUPS, GM_CHUNK, GM_CHUNK))
    y, v = _gmlp(x.reshape(n, D_MODEL), g_mix[None, :], w_in.astype(BF16), v_g[None, :], ws, bs,
                 w_out.astype(BF16), tm=tm, tc=tc, emit_v=emit_v)
    return y.reshape(b, t, D_MODEL), (v.reshape(b, t, D_MODEL) if emit_v else None)


def _ffn_ple_layer(x, p_all, layer, past, g_ffn, w_up, conv_w, conv_b, w_down, g_ple, w_gate, w_proj, *, tm):
    b, t, _ = x.shape
    n = b * t
    y, state = _conv_ffn_ple(x.reshape(n, D_MODEL), p_all.reshape(-1, p_all.shape[-1]), g_ffn[None, :],
                             w_up.astype(BF16), conv_w, conv_b[None, :], w_down.astype(BF16),
                             g_ple[None, :], w_gate.astype(BF16), w_proj.astype(BF16), past,
                             nb=b, tm=tm, layer=layer)
    return y.reshape(b, t, D_MODEL), state


def kernel(x_prompt, x_sample, cache_k, cache_v, cache_kidx, state_ffn_conv, p_prompt, p_sample,
           norm_mix, attn_w_in, attn_q_norm, attn_k_norm, attn_w_out,
           gmlp_w_in, gmlp_v_norm, gmlp_w_spatial, gmlp_b_spatial, gmlp_w_out,
           norm_ffn, ffn_w_up, ffn_conv_w, ffn_conv_b, ffn_w_down,
           norm_ple, ple_w_gate, ple_w_proj):
    depth = norm_mix.shape[0]
    t_s = x_sample.shape[1]
    n_s = x_sample.shape[0] * t_s
    yp, ys = x_prompt, x_sample
    kp, vp, kip, ks, vs, kis, gvs, cps, css = [], [], [], [], [], [], [], [], []
    for i in range(depth):
        j = i // 2
        if i % 2 == 0:
            yp, k, v, ki = _attn_layer(yp, norm_mix[i], attn_w_in[j], attn_q_norm[j], attn_k_norm[j],
                                       attn_w_out[j], None, tm=512, tq=256)
            kp.append(k); vp.append(v); kip.append(ki)
            ys, k, v, ki = _attn_layer(ys, norm_mix[i], attn_w_in[j], attn_q_norm[j], attn_k_norm[j],
                                       attn_w_out[j], (cache_k[j], cache_v[j], cache_kidx[j]), tm=n_s, tq=128)
            ks.append(k); vs.append(v); kis.append(ki)
        else:
            yp, _ = _gmlp_layer(yp, norm_mix[i], gmlp_w_in[j], gmlp_v_norm[j], gmlp_w_spatial[j],
                                gmlp_b_spatial[j], gmlp_w_out[j], tm=512, emit_v=False)
            ys, gv = _gmlp_layer(ys, norm_mix[i], gmlp_w_in[j], gmlp_v_norm[j], gmlp_w_spatial[j],
                                 gmlp_b_spatial[j], gmlp_w_out[j], tm=n_s, emit_v=True)
            gvs.append(gv)
        ffn = (norm_ffn[i], ffn_w_up[i], ffn_conv_w[i], ffn_conv_b[i], ffn_w_down[i],
               norm_ple[i], ple_w_gate[i], ple_w_proj[i])
        yp, cp = _ffn_ple_layer(yp, p_prompt, i, None, *ffn, tm=512)
        ys, cs = _ffn_ple_layer(ys, p_sample, i, state_ffn_conv[i], *ffn, tm=t_s)
        cps.append(cp); css.append(cs)
    return (yp, ys, jnp.stack(kp, 0), jnp.stack(vp, 0), jnp.stack(kip, 0),
            jnp.stack(ks, 0), jnp.stack(vs, 0), jnp.stack(kis, 0), jnp.stack(gvs, 0),
            jnp.stack(cps, 0), jnp.stack(css, 0))
```

```python
import functools

import numpy as np
import jax
import jax.numpy as jnp
from jax import lax
from jax.experimental import pallas as pl
from jax.experimental.pallas import tpu as pltpu

F32 = jnp.float32
BF16 = jnp.bfloat16
I32 = jnp.int32
I16 = jnp.int16

D_MODEL = 1024
N_HEADS = 8
HEAD_DIM = 128
N_KV_HEADS = 2
N_IDX_HEADS = 8
IDX_DIM = 64
CHUNK = 64
TOPK_MAX = 256
ROPE_THETA = 500000.0
ROPE_FRAC_DIV = 4
GM_GROUPS = 8
GM_CHUNK = 128
D_FF = 2816
CONV_W = 3
EPS = 1e-6

O_Q = N_HEADS * HEAD_DIM
O_K = O_Q + N_KV_HEADS * HEAD_DIM
O_V = O_K + N_KV_HEADS * HEAD_DIM
O_QI = O_V + N_IDX_HEADS * IDX_DIM
O_KI = O_QI + IDX_DIM
ATTN_PROJ = O_KI + N_IDX_HEADS

LANES = 128
ATTN_PROJ_PAD = 2176
INT_MIN = -2147483648
I16_MIN = -32768
MASK_BIAS = -1e30
LOGIT_SCALE = (HEAD_DIM ** -0.5) * 1.4426950408889634
KEY_TILE = 256
COUNT_ROWS = 64
Q_ROWS = 128
ATTN_UNROLL = 4
VMEM_LIMIT = 56 * 1024 * 1024
RESIDENT = pl.Buffered(1)


def _cparams(sem, flags=None):
    return pltpu.CompilerParams(dimension_semantics=sem, vmem_limit_bytes=VMEM_LIMIT, flags=flags)


def _nt_dot(a, b):
    return lax.dot_general(a, b, (((1,), (1,)), ((), ())), preferred_element_type=F32)


def _rms(x):
    return x * lax.rsqrt(jnp.mean(x * x, axis=-1, keepdims=True) + EPS)


def _rope_tables(pos, head_w):
    rot = head_w // ROPE_FRAC_DIV
    half = rot // 2
    inv = ROPE_THETA ** (-jnp.arange(half, dtype=F32) * (2.0 / rot))
    ang = pos.astype(F32)[:, None] * inv[None, :]
    cos, sin = jnp.cos(ang), jnp.sin(ang)
    j = np.arange(LANES) % head_w
    idx = j % half
    first = jnp.asarray(j < half)[None, :]
    second = jnp.asarray((j >= half) & (j < rot))[None, :]
    cos_t = jnp.where(first | second, cos[:, idx], 1.0)
    sin_t = jnp.where(first, -sin[:, idx], jnp.where(second, sin[:, idx], 0.0))
    return cos_t.astype(F32), sin_t.astype(F32)


T_Q, T_K, T_V, T_QI, T_KI, T_W = 0, 8, 10, 12, 16, 17
T_KP, T_QIP, T_KIP = 18, 20, 24
PROJ_TILES = 25
PROJ_COLS = PROJ_TILES * LANES
ROPE_HALF = HEAD_DIM // ROPE_FRAC_DIV // 2


def _partner_columns(base, width):
    j = np.arange(LANES)
    half = width // ROPE_FRAC_DIV // 2
    jj = j % width
    out = np.full(LANES, ATTN_PROJ, np.int64)
    out[jj < half] = (base + j + half)[jj < half]
    sec = (jj >= half) & (jj < 2 * half)
    out[sec] = (base + j - half)[sec]
    return out


def _proj_column_sources():
    src = np.full((PROJ_TILES, LANES), ATTN_PROJ, np.int64)
    j = np.arange(LANES)
    for t in range(T_KI):
        src[t] = t * LANES + j
    src[T_KI] = O_QI + j % IDX_DIM
    src[T_W, :N_IDX_HEADS] = O_KI + j[:N_IDX_HEADS]
    for t in range(N_KV_HEADS):
        src[T_KP + t] = _partner_columns(O_Q + t * HEAD_DIM, HEAD_DIM)
    for t in range(T_KI - T_QI):
        src[T_QIP + t] = _partner_columns(O_V + t * LANES, IDX_DIM)
    kip = _partner_columns(0, IDX_DIM)
    src[T_KIP] = np.where(kip == ATTN_PROJ, ATTN_PROJ, O_QI + kip % IDX_DIM)
    return src.reshape(-1)


def _attn_proj_body(x_ref, gm_ref, w_ref, qg_ref, kg_ref, kgp_ref, cq_ref, sq_ref, ci_ref, si_ref,
                    q_ref, kf_ref, vf_ref, kif_ref, kt_ref, vb_ref, kic_ref, qit_ref, wt_ref,
                    z_ref, *, tm):
    x = x_ref[...]
    h = _rms(x) * gm_ref[...]
    z_ref[...] = jnp.dot(h.astype(BF16), w_ref[...], preferred_element_type=F32)

    rb = LANES
    w_scale = (N_IDX_HEADS ** -0.5) * (IDX_DIM ** -0.5)
    first_half = lax.broadcasted_iota(I32, (rb, LANES), 1) < ROPE_HALF

    def tile(rows, t):
        return z_ref[rows, t * LANES:(t + 1) * LANES]

    def normed_rope(rows, t, tp, g_ref, gp_ref, c, s):
        z = tile(rows, t)
        rs = lax.rsqrt(jnp.mean(z * z, axis=-1, keepdims=True) + EPS)
        y = z * rs * g_ref[...]
        if tp is None:
            yp = jnp.where(first_half, pltpu.roll(y, LANES - ROPE_HALF, 1), pltpu.roll(y, ROPE_HALF, 1))
        else:
            yp = tile(rows, tp) * rs * gp_ref[...]
        return y * c + yp * s

    def split(y):
        hi = y.astype(BF16).astype(F32)
        return hi, y - hi

    for r in range(tm // rb):
        rows = slice(r * rb, (r + 1) * rb)
        cq, sq = cq_ref[rows, :], sq_ref[rows, :]
        ci, si = ci_ref[rows, :], si_ref[rows, :]
        for hd in range(N_HEADS):
            cols = slice(hd * HEAD_DIM, (hd + 1) * HEAD_DIM)
            y = normed_rope(rows, T_Q + hd, None, qg_ref, None, cq, sq)
            q_ref[rows, cols] = (y * LOGIT_SCALE).astype(BF16)
        for g in range(N_KV_HEADS):
            cols = slice(g * HEAD_DIM, (g + 1) * HEAD_DIM)
            y = normed_rope(rows, T_K + g, T_KP + g, kg_ref, kgp_ref, cq, sq)
            kf_ref[rows, g, :] = y
            kt_ref[r, cols, :] = y.T.astype(BF16)
            v = tile(rows, T_V + g)
            vf_ref[rows, g, :] = v
            vb_ref[rows, cols] = v.astype(BF16)
        for t in range(T_KI - T_QI):
            y = tile(rows, T_QI + t) * ci + tile(rows, T_QIP + t) * si
            hi, lo = split(y)
            hi_t, lo_t = hi.T.astype(BF16), lo.T.astype(BF16)
            for u in range(2):
                base = (2 * t + u) * LANES
                dims = slice(u * IDX_DIM, (u + 1) * IDX_DIM)
                qit_ref[r, base:base + IDX_DIM, :] = hi_t[dims, :]
                qit_ref[r, base + IDX_DIM:base + LANES, :] = lo_t[dims, :]
        y = tile(rows, T_KI) * ci + tile(rows, T_KIP) * si
        kif_ref[rows, :] = y[:, :IDX_DIM]
        hi, lo = split(y)
        kic_ref[rows, 0:LANES] = hi.astype(BF16)
        kic_ref[rows, LANES:2 * LANES] = lo.astype(BF16)
        wt_ref[:, rows] = tile(rows, T_W).T[:N_IDX_HEADS, :] * w_scale


def _attn_proj(x, g_mix, w_in, gains, tabs, *, tm, n_tab):
    n = x.shape[0]
    cq, sq, ci, si = tabs
    row = lambda i: (i, 0)
    tab = lambda i: (i % n_tab, 0)
    const = lambda i: (0, 0)
    out_shape = (
        jax.ShapeDtypeStruct((n, O_Q), BF16),
        jax.ShapeDtypeStruct((n, N_KV_HEADS, HEAD_DIM), F32),
        jax.ShapeDtypeStruct((n, N_KV_HEADS, HEAD_DIM), F32),
        jax.ShapeDtypeStruct((n, IDX_DIM), F32),
        jax.ShapeDtypeStruct((n // LANES, N_KV_HEADS * HEAD_DIM, LANES), BF16),
        jax.ShapeDtypeStruct((n, N_KV_HEADS * HEAD_DIM), BF16),
        jax.ShapeDtypeStruct((n, 2 * LANES), BF16),
        jax.ShapeDtypeStruct((n // LANES, N_IDX_HEADS * LANES, LANES), BF16),
        jax.ShapeDtypeStruct((N_IDX_HEADS, n), F32),
    )
    tile3 = lambda i: (i, 0, 0)
    rspec = lambda w: pl.BlockSpec((tm, w), row)
    kvspec = pl.BlockSpec((tm, N_KV_HEADS, HEAD_DIM), tile3)
    out_specs = (rspec(O_Q), kvspec, kvspec, rspec(IDX_DIM),
                 pl.BlockSpec((tm // LANES, N_KV_HEADS * HEAD_DIM, LANES), tile3),
                 rspec(256), rspec(2 * LANES),
                 pl.BlockSpec((tm // LANES, N_IDX_HEADS * LANES, LANES), tile3),
                 pl.BlockSpec((N_IDX_HEADS, tm), lambda i: (0, i)))
    return pl.pallas_call(
        functools.partial(_attn_proj_body, tm=tm),
        out_shape=out_shape,
        grid=(n // tm,),
        in_specs=[
            pl.BlockSpec((tm, D_MODEL), row),
            pl.BlockSpec((1, D_MODEL), const),
            pl.BlockSpec((D_MODEL, PROJ_COLS), const, pipeline_mode=RESIDENT),
            pl.BlockSpec((1, HEAD_DIM), const),
            pl.BlockSpec((1, HEAD_DIM), const),
            pl.BlockSpec((1, HEAD_DIM), const),
            pl.BlockSpec((tm, LANES), tab),
            pl.BlockSpec((tm, LANES), tab),
            pl.BlockSpec((tm, LANES), tab),
            pl.BlockSpec((tm, LANES), tab),
        ],
        out_specs=out_specs,
        scratch_shapes=[pltpu.VMEM((tm, PROJ_COLS), F32)],
        compiler_params=_cparams(("parallel",)),
        name="attn_proj",
    )(x, g_mix, w_in, *gains, cq, sq, ci, si)


def _sparse_attn_body(x_ref, q_ref, qit_ref, wt_ref, ki_ref, kt_ref, v_ref, wo_ref, y_ref,
                      keys_ref, hi_ref, lo_ref, bias_ref, sacc_ref, qs_ref, s_ref, p_ref, alpha_ref, m_ref,
                      acc_ref, o_ref, *, tq, pos0, l_true, n_valid, k_sel):
    j = pl.program_id(1)
    q0 = pos0 + j * tq
    nqt = tq // LANES
    lane_q = lax.broadcasted_iota(I32, (1, tq), 1)
    qpos = q0 + lane_q
    limit = jnp.minimum((lax.shift_right_logical(qpos, 6) + 1) * CHUNK, l_true)
    lim_max = jnp.minimum((lax.shift_right_logical(q0 + tq - 1, 6) + 1) * CHUNK, l_true)
    n_kt = lax.shift_right_logical(lim_max + KEY_TILE - 1, 8)

    def score_rows(r0, nrows, slot):
        rows = pl.ds(r0, nrows)
        kt = ki_ref[0, rows, :]
        for h in range(N_IDX_HEADS):
            qh = jnp.concatenate([qit_ref[u, h * LANES:(h + 1) * LANES, :] for u in range(nqt)], axis=1)
            s = jnp.dot(kt, jnp.concatenate([qh, qh], axis=0), preferred_element_type=F32)
            term = wt_ref[h:h + 1, :] * jnp.maximum(s, 0.0)
            if h == 0:
                sacc_ref[slot, 0:nrows, :] = term
            elif h < N_IDX_HEADS - 1:
                sacc_ref[slot, 0:nrows, :] += term
            else:
                bits = pltpu.bitcast(sacc_ref[slot, 0:nrows, :] + term, I32)
                key = jnp.where(bits < 0, INT_MIN - bits, bits)
                key = jnp.where(r0 + lax.broadcasted_iota(I32, (nrows, tq), 0) < limit, key, INT_MIN)
                keys_ref[rows, :] = key
                hi_ref[rows, :] = lax.shift_right_arithmetic(key, 16).astype(I16)
                lo_ref[rows, :] = ((key & 0xFFFF) + I16_MIN).astype(I16)

    def score_quad(i, carry):
        r0 = pl.multiple_of(i * 4 * KEY_TILE, 4 * KEY_TILE)
        score_rows(r0, 2 * KEY_TILE, 0)
        score_rows(r0 + 2 * KEY_TILE, 2 * KEY_TILE, 1)
        return carry

    def score_pair(i, carry):
        score_rows(pl.multiple_of((n_kt & -4) * KEY_TILE, 2 * KEY_TILE), 2 * KEY_TILE, 0)
        return carry

    def score_last(i, carry):
        score_rows(pl.multiple_of((n_kt - 1) * KEY_TILE, KEY_TILE), KEY_TILE, 0)
        return carry

    lax.fori_loop(0, lax.shift_right_logical(n_kt, 2), score_quad, 0)
    lax.fori_loop(0, lax.shift_right_logical(n_kt, 1) & 1, score_pair, 0)
    lax.fori_loop(0, n_kt & 1, score_last, 0)

    def count(pred):
        def body(t, acc):
            for u in range(KEY_TILE // COUNT_ROWS):
                r0 = pl.multiple_of(t * KEY_TILE + u * COUNT_ROWS, COUNT_ROWS)
                acc = acc + jnp.where(pred(keys_ref[pl.ds(r0, COUNT_ROWS), :], r0), 1.0, 0.0)
            return acc
        acc = lax.fori_loop(0, n_kt, body, jnp.zeros((COUNT_ROWS, tq), F32))
        return jnp.sum(acc, axis=0, keepdims=True)

    def count16(ref, cand):
        c16 = jnp.broadcast_to(cand, (COUNT_ROWS, tq)).astype(I16)

        def body(t, acc):
            for u in range(KEY_TILE // COUNT_ROWS):
                r0 = pl.multiple_of(t * KEY_TILE + u * COUNT_ROWS, COUNT_ROWS)
                acc = acc + jnp.where(ref[pl.ds(r0, COUNT_ROWS), :] >= c16, jnp.int16(1), jnp.int16(0))
            return acc
        acc = lax.fori_loop(0, n_kt, body, jnp.zeros((COUNT_ROWS, tq), I16))
        return jnp.sum(acc.astype(I32).astype(F32), axis=0, keepdims=True)

    def search16(ref, need):
        def step(p, ans):
            cand = ans + lax.shift_left(jnp.int32(1), 15 - p)
            return jnp.where(count16(ref, cand) >= need, cand, ans)
        return lax.fori_loop(0, 16, step, jnp.full((1, tq), I16_MIN, I32))

    kf = float(k_sel)
    hi_k = search16(hi_ref, kf)
    n_above = jnp.where(hi_k >= -I16_MIN - 1, 0.0, count16(hi_ref, hi_k + 1))
    hi_k16 = jnp.broadcast_to(hi_k, (KEY_TILE, tq)).astype(I16)

    def keep_lo(t, carry):
        r0 = pl.multiple_of(t * KEY_TILE, KEY_TILE)
        rows = pl.ds(r0, KEY_TILE)
        lo_ref[rows, :] = jnp.where(hi_ref[rows, :] == hi_k16, lo_ref[rows, :], jnp.int16(I16_MIN))
        return carry

    lax.fori_loop(0, n_kt, keep_lo, 0)
    lo_k = search16(lo_ref, kf - n_above)
    thr = hi_k * 65536 + (lo_k - I16_MIN)
    thr_c = jnp.maximum(thr, INT_MIN + 1)
    n_ge = count(lambda c, r0: c >= thr_c)

    def to_rows(v):
        b = jnp.broadcast_to(v, (LANES, tq))
        col = jnp.concatenate([b[:, u * LANES:(u + 1) * LANES].T for u in range(nqt)], axis=0)
        return jnp.concatenate([col] * (KEY_TILE // LANES), axis=1)

    def write_bias(selected):
        def tile(t):
            r0 = pl.multiple_of(t * KEY_TILE, KEY_TILE)
            kq = keys_ref[pl.ds(r0, KEY_TILE), :].T
            bias_ref[t] = jnp.where(selected(kq, r0), 0.0, MASK_BIAS)

        def pair(i, carry):
            tile(2 * i)
            tile(2 * i + 1)
            return carry

        def last(i, carry):
            tile(n_kt - 1)
            return carry

        lax.fori_loop(0, lax.shift_right_logical(n_kt, 1), pair, 0)
        lax.fori_loop(0, n_kt & 1, last, 0)

    thr_rows = to_rows(thr_c)
    write_bias(lambda kq, r0: kq >= thr_rows)

    excess = jnp.where((n_ge > kf) & (lane_q < n_valid), 1.0, 0.0)

    @pl.when(jnp.max(excess) > 0.0)
    def _():
        need = kf - count(lambda c, r0: c > thr)
        idx_iota = lax.broadcasted_iota(I32, (COUNT_ROWS, tq), 0)
        last = jnp.zeros((1, tq), I32)
        for bit in range(13, -1, -1):
            cand = last | (1 << bit)
            cnt = count(lambda c, r0: (c == thr) & (r0 + idx_iota < cand))
            last = jnp.where(cnt < need, cand, last)
        key_pos = lax.broadcasted_iota(I32, (tq, KEY_TILE), 1)
        t_rows, l_rows = to_rows(thr), to_rows(last)
        write_bias(lambda kq, r0: (kq > t_rows) | ((kq == t_rows) & (kq > INT_MIN) & (r0 + key_pos <= l_rows)))

    hpg = N_HEADS // N_KV_HEADS
    for g in range(N_KV_HEADS):
        qs_ref[g] = jnp.concatenate(
            [q_ref[0, :, (g * hpg + hh) * HEAD_DIM:(g * hpg + hh + 1) * HEAD_DIM] for hh in range(hpg)], axis=0)
    m_ref[...] = jnp.full(m_ref.shape, MASK_BIAS, F32)
    acc_ref[...] = jnp.zeros(acc_ref.shape, F32)
    ones = jnp.ones((KEY_TILE, HEAD_DIM), BF16)

    def attn_tile(t, slot):
        r0 = pl.multiple_of(t * KEY_TILE, KEY_TILE)
        for g in range(N_KV_HEADS):
            gcols = slice(g * HEAD_DIM, (g + 1) * HEAD_DIM)
            kt = jnp.concatenate([kt_ref[2 * t + u, gcols, :] for u in range(KEY_TILE // LANES)], axis=1)
            s_ref[slot] = jnp.dot(qs_ref[g], kt, preferred_element_type=F32)
            for rb in range(hpg * nqt):
                rows = slice(rb * Q_ROWS, (rb + 1) * Q_ROWS)
                qrows = slice((rb % nqt) * Q_ROWS, (rb % nqt + 1) * Q_ROWS)
                s = s_ref[slot, rows, :] + bias_ref[t, qrows, :]
                s0, s1 = s[:, :LANES], s[:, LANES:]
                m_old = m_ref[g, rows, :]
                m_new = jnp.maximum(m_old, jnp.max(jnp.maximum(s0, s1), axis=-1, keepdims=True))
                alpha_ref[slot, rows, :] = jnp.exp2(m_old - m_new)
                m_ref[g, rows, :] = m_new
                p_ref[slot, rows, :LANES] = jnp.exp2(s0 - m_new).astype(BF16)
                p_ref[slot, rows, LANES:] = jnp.exp2(s1 - m_new).astype(BF16)
            vt = jnp.concatenate([v_ref[0, pl.ds(r0, KEY_TILE), gcols], ones], axis=1)
            a = alpha_ref[slot]
            acc_ref[g] = acc_ref[g] * jnp.concatenate([a, a], axis=1) + jnp.dot(
                p_ref[slot], vt, preferred_element_type=F32)

    def attn_group(first_tile, width):
        def body(i, carry):
            for u in range(width):
                attn_tile(first_tile + i * width + u, u)
            return carry
        return body

    n4 = lax.shift_right_logical(n_kt, 2)
    n2 = lax.shift_right_logical(n_kt, 1) & 1
    lax.fori_loop(0, n4, attn_group(0, ATTN_UNROLL), 0)
    lax.fori_loop(0, n2, attn_group(n4 * 4, 2), 0)
    lax.fori_loop(0, n_kt & 1, attn_group(n_kt - 1, 1), 0)
    for h in range(N_HEADS):
        g, hh = divmod(h, hpg)
        rows = slice(hh * tq, (hh + 1) * tq)
        o_ref[:, h * HEAD_DIM:(h + 1) * HEAD_DIM] = (
            acc_ref[g, rows, :HEAD_DIM] / acc_ref[g, rows, HEAD_DIM:]).astype(BF16)
    y_ref[0] = x_ref[0] + jnp.dot(o_ref[...], wo_ref[...], preferred_element_type=F32)


def _sparse_attn(x, q, qit, wt, ki, kt, v, w_out, *, tq, pos0, l_true, n_valid):
    b, t_q, _ = q.shape
    lp = v.shape[1]
    nq = t_q // tq
    nqt = tq // LANES
    k_sel = min(TOPK_MAX, l_true // 4)
    hpg = N_HEADS // N_KV_HEADS
    ar = hpg * tq
    body = functools.partial(_sparse_attn_body, tq=tq, pos0=pos0, l_true=l_true,
                             n_valid=n_valid, k_sel=k_sel)
    xspec = pl.BlockSpec((1, tq, D_MODEL), lambda i, j: (i, j, 0))
    qspec = pl.BlockSpec((1, tq, O_Q), lambda i, j: (i, j, 0))
    kspec = pl.BlockSpec((1, lp, 2 * LANES), lambda i, j: (i, 0, 0))
    return pl.pallas_call(
        body,
        out_shape=jax.ShapeDtypeStruct((b, t_q, D_MODEL), F32),
        grid=(b, nq),
        in_specs=[xspec, qspec,
                  pl.BlockSpec((nqt, N_IDX_HEADS * LANES, LANES), lambda i, j: (i * nq + j, 0, 0)),
                  pl.BlockSpec((N_IDX_HEADS, tq), lambda i, j: (0, i * nq + j)),
                  kspec,
                  pl.BlockSpec((lp // LANES, 2 * LANES, LANES), lambda i, j: (i, 0, 0)),
                  kspec,
                  pl.BlockSpec((O_Q, D_MODEL), lambda i, j: (0, 0))],
        out_specs=xspec,
        scratch_shapes=[pltpu.VMEM((lp, tq), I32), pltpu.VMEM((lp, tq), I16), pltpu.VMEM((lp, tq), I16),
                        pltpu.VMEM((lp // KEY_TILE, tq, KEY_TILE), F32),
                        pltpu.VMEM((2, 2 * KEY_TILE, tq), F32),
                        pltpu.VMEM((N_KV_HEADS, ar, HEAD_DIM), BF16),
                        pltpu.VMEM((ATTN_UNROLL, ar, KEY_TILE), F32),
                        pltpu.VMEM((ATTN_UNROLL, ar, KEY_TILE), BF16),
                        pltpu.VMEM((ATTN_UNROLL, ar, LANES), F32),
                        pltpu.VMEM((N_KV_HEADS, ar, LANES), F32),
                        pltpu.VMEM((N_KV_HEADS, ar, 2 * HEAD_DIM), F32),
                        pltpu.VMEM((tq, O_Q), BF16)],
        compiler_params=_cparams(("parallel", "arbitrary")),
        name="sparse_attn",
    )(x, q, qit, wt, ki, kt, v, w_out)


def _gmlp_body(x_ref, gm_ref, w_in_ref, vg_ref, ws_ref, bs_ref, w_out_ref, *rest, tm, tc, emit_v):
    if emit_v:
        y_ref, v_ref, z_ref, s_ref = rest
    else:
        y_ref, z_ref, s_ref = rest
    x = x_ref[...]
    h = _rms(x) * gm_ref[...]
    hb = h.astype(BF16)
    z_ref[:, D_MODEL:] = jax.nn.gelu(jnp.dot(hb, w_in_ref[:, D_MODEL:], preferred_element_type=F32))
    z_ref[:, :D_MODEL] = jax.nn.gelu(jnp.dot(hb, w_in_ref[:, :D_MODEL], preferred_element_type=F32))
    v = _rms(z_ref[:, D_MODEL:]) * vg_ref[...]
    if emit_v:
        v_ref[...] = v
    z_ref[:, D_MODEL:] = v
    gw = D_MODEL // GM_GROUPS
    r_i = lax.broadcasted_iota(I32, (GM_CHUNK, GM_CHUNK), 0)
    c_i = lax.broadcasted_iota(I32, (GM_CHUNK, GM_CHUNK), 1)
    keep = (r_i // tc == c_i // tc) & (c_i <= r_i)
    for g in range(GM_GROUPS):
        ws = jnp.where(keep, ws_ref[g], 0.0).astype(BF16)
        bias = bs_ref[g]
        cols = slice(g * gw, (g + 1) * gw)
        for ch in range(tm // GM_CHUNK):
            rows = slice(ch * GM_CHUNK, (ch + 1) * GM_CHUNK)
            vb = z_ref[rows, D_MODEL + g * gw:D_MODEL + (g + 1) * gw].astype(BF16)
            mixed = jnp.dot(ws, vb, preferred_element_type=F32) + bias
            s_ref[rows, cols] = (z_ref[rows, cols] * mixed).astype(BF16)
    y_ref[...] = x + jnp.dot(s_ref[...], w_out_ref[...], preferred_element_type=F32)


def _gmlp(x, g_mix, w_in, v_g, ws, bs, w_out, *, tm, tc, emit_v):
    n = x.shape[0]
    row = lambda i: (i, 0)
    const2 = lambda i: (0, 0)
    out_shape = [jax.ShapeDtypeStruct((n, D_MODEL), F32)]
    out_specs = [pl.BlockSpec((tm, D_MODEL), row)]
    if emit_v:
        out_shape.append(jax.ShapeDtypeStruct((n, D_MODEL), F32))
        out_specs.append(pl.BlockSpec((tm, D_MODEL), row))
    res = pl.pallas_call(
        functools.partial(_gmlp_body, tm=tm, tc=tc, emit_v=emit_v),
        out_shape=tuple(out_shape),
        grid=(n // tm,),
        in_specs=[pl.BlockSpec((tm, D_MODEL), row), pl.BlockSpec((1, D_MODEL), const2),
                  pl.BlockSpec((D_MODEL, 2 * D_MODEL), const2, pipeline_mode=RESIDENT),
                  pl.BlockSpec((1, D_MODEL), const2),
                  pl.BlockSpec((GM_GROUPS, GM_CHUNK, GM_CHUNK), lambda i: (0, 0, 0)),
                  pl.BlockSpec((GM_GROUPS, GM_CHUNK, GM_CHUNK), lambda i: (0, 0, 0)),
                  pl.BlockSpec((D_MODEL, D_MODEL), const2, pipeline_mode=RESIDENT)],
        out_specs=tuple(out_specs),
        scratch_shapes=[pltpu.VMEM((tm, 2 * D_MODEL), F32), pltpu.VMEM((tm, D_MODEL), BF16)],
        compiler_params=_cparams(("parallel",)),
        name="gmlp",
    )(x, g_mix, w_in, v_g, ws, bs, w_out)
    return res if emit_v else (res[0], None)


HALO = 8


FFN_COLS = 256
FFN_ROWS = 128


def _ffn_body(*refs, tm, nt, has_past):
    if has_past:
        (x_ref, gn_ref, wup_ref, cw_ref, cb_ref, wd_ref, p_ref, gp_ref, wgate_ref, wproj_ref, past_ref,
         y_ref, st_ref, hb_ref, stage_ref, s_ref, car_ref) = refs
    else:
        (x_ref, gn_ref, wup_ref, cw_ref, cb_ref, wd_ref, p_ref, gp_ref, wgate_ref, wproj_ref,
         y_ref, st_ref, hb_ref, stage_ref, s_ref, car_ref) = refs
        past_ref = None
    first = lax.rem(pl.program_id(0), nt) == 0

    @pl.when(pl.program_id(0) == 0)
    def _():
        car_ref[...] = jnp.zeros(car_ref.shape, F32)

    x = x_ref[...]
    hb_ref[...] = (_rms(x) * gn_ref[...]).astype(BF16)
    rb = min(tm, FFN_ROWS)
    n_stage = D_FF // FFN_COLS

    def cols_of(k, br):
        return slice(br * D_FF + k * FFN_COLS, br * D_FF + (k + 1) * FFN_COLS)

    def up_stage(k):
        for br in range(2):
            cols = cols_of(k, br)
            if past_ref is None:
                init = jnp.zeros((CONV_W - 1, FFN_COLS), F32)
            else:
                init = past_ref[0, :, cols]
            stage_ref[k % 2, br, HALO - 2:HALO, :] = jnp.where(first, init, car_ref[0:CONV_W - 1, cols])
            stage_ref[k % 2, br, HALO:HALO + tm, :] = jnp.dot(
                hb_ref[...], wup_ref[:, cols], preferred_element_type=F32)

    def conv_stage(k):
        slot = k % 2
        for br in range(2):
            tail = stage_ref[slot, br, HALO + tm - 2:HALO + tm, :]
            car_ref[0:CONV_W - 1, cols_of(k, br)] = tail
            st_ref[0, :, cols_of(k, br)] = tail
        for r in range(tm // rb):
            for cc in range(FFN_COLS // LANES):
                lanes = slice(cc * LANES, (cc + 1) * LANES)

                def conv(br):
                    c0 = br * D_FF + k * FFN_COLS + cc * LANES
                    wcol = slice(c0, c0 + LANES)
                    r0 = HALO + r * rb
                    return (cb_ref[:, wcol]
                            + cw_ref[2:3, wcol] * stage_ref[slot, br, r0:r0 + rb, lanes]
                            + cw_ref[1:2, wcol] * stage_ref[slot, br, r0 - 1:r0 - 1 + rb, lanes]
                            + cw_ref[0:1, wcol] * stage_ref[slot, br, r0 - 2:r0 - 2 + rb, lanes])

                gate = conv(0)
                s_ref[k % 3, r * rb:(r + 1) * rb, lanes] = (gate * jax.nn.sigmoid(gate) * conv(1)).astype(BF16)

    def down_stage(k):
        part = jnp.dot(s_ref[k % 3], wd_ref[k * FFN_COLS:(k + 1) * FFN_COLS, :], preferred_element_type=F32)
        if k == 0:
            y_ref[...] = x + part
        else:
            y_ref[...] += part

    up_stage(0)
    for k in range(n_stage):
        if k + 1 < n_stage:
            up_stage(k + 1)
        conv_stage(k)
        if k > 0:
            down_stage(k - 1)
    down_stage(n_stage - 1)

    y = y_ref[...]
    h = (_rms(y) * gp_ref[...]).astype(BF16)
    gate = jax.nn.sigmoid(jnp.dot(h, wgate_ref[...], preferred_element_type=F32))
    proj = jnp.dot(p_ref[...].astype(BF16), wproj_ref[...], preferred_element_type=F32)
    y_ref[...] = y + gate * proj


def _conv_ffn_ple(x, p, g_norm, w_up, conv_w, conv_b, w_down, g_ple, w_gate, w_proj, past, *, nb, tm, layer):
    n = x.shape[0]
    pd = p.shape[1]
    nt = n // nb // tm
    has_past = past is not None
    row = lambda i: (i, 0)
    p_row = lambda i: (layer * (n // tm) + i, 0)
    const = lambda i: (0, 0)
    in_specs = [
        pl.BlockSpec((tm, D_MODEL), row),
        pl.BlockSpec((1, D_MODEL), const),
        pl.BlockSpec((D_MODEL, 2 * D_FF), const, pipeline_mode=RESIDENT),
        pl.BlockSpec((CONV_W, 2 * D_FF), const),
        pl.BlockSpec((1, 2 * D_FF), const),
        pl.BlockSpec((D_FF, D_MODEL), const, pipeline_mode=RESIDENT),
        pl.BlockSpec((tm, pd), p_row),
        pl.BlockSpec((1, D_MODEL), const),
        pl.BlockSpec((D_MODEL, D_MODEL), const, pipeline_mode=RESIDENT),
        pl.BlockSpec((pd, D_MODEL), const, pipeline_mode=RESIDENT),
    ]
    args = [x, g_norm, w_up, conv_w, conv_b, w_down, p, g_ple, w_gate, w_proj]
    if has_past:
        in_specs.append(pl.BlockSpec((1, CONV_W - 1, 2 * D_FF), lambda i: (i // nt, 0, 0)))
        args.append(past)
    y, tails = pl.pallas_call(
        functools.partial(_ffn_body, tm=tm, nt=nt, has_past=has_past),
        out_shape=(jax.ShapeDtypeStruct((n, D_MODEL), F32),
                   jax.ShapeDtypeStruct((n // tm, CONV_W - 1, 2 * D_FF), F32)),
        grid=(n // tm,),
        in_specs=in_specs,
        out_specs=(pl.BlockSpec((tm, D_MODEL), row),
                   pl.BlockSpec((1, CONV_W - 1, 2 * D_FF), lambda i: (i, 0, 0))),
        scratch_shapes=[pltpu.VMEM((tm, D_MODEL), BF16),
                        pltpu.VMEM((2, 2, tm + HALO, FFN_COLS), F32),
                        pltpu.VMEM((3, tm, FFN_COLS), BF16),
                        pltpu.VMEM((HALO, 2 * D_FF), F32)],
        compiler_params=_cparams(("arbitrary",)),
        name="conv_ffn_ple",
    )(*args)
    return y, tails.reshape(nb, nt, CONV_W - 1, 2 * D_FF)[:, nt - 1]


def _cache_prep_body(ck_ref, cv_ref, cki_ref, kt_ref, v_ref, ki_ref, *, tm):
    for r in range(tm // LANES):
        rows = slice(r * LANES, (r + 1) * LANES)
        for g in range(N_KV_HEADS):
            cols = slice(g * HEAD_DIM, (g + 1) * HEAD_DIM)
            kt_ref[r, cols, :] = ck_ref[rows, g, :].T.astype(BF16)
            v_ref[rows, cols] = cv_ref[rows, g, :].astype(BF16)
        ki = cki_ref[rows, :]
        hi = ki.astype(BF16).astype(F32)
        ki_ref[rows, :] = jnp.concatenate([hi, hi, ki - hi, ki - hi], axis=1).astype(BF16)


def _cache_prep(ck, cv, cki, *, tm):
    n = ck.shape[0]
    kvw = N_KV_HEADS * HEAD_DIM
    kv_in = pl.BlockSpec((tm, N_KV_HEADS, HEAD_DIM), lambda i: (i, 0, 0))
    row = lambda i: (i, 0)
    return pl.pallas_call(
        functools.partial(_cache_prep_body, tm=tm),
        out_shape=(jax.ShapeDtypeStruct((n // LANES, kvw, LANES), BF16),
                   jax.ShapeDtypeStruct((n, kvw), BF16),
                   jax.ShapeDtypeStruct((n, 2 * LANES), BF16)),
        grid=(n // tm,),
        in_specs=[kv_in, kv_in, pl.BlockSpec((tm, IDX_DIM), row)],
        out_specs=(pl.BlockSpec((tm // LANES, kvw, LANES), lambda i: (i, 0, 0)),
                   pl.BlockSpec((tm, kvw), row), pl.BlockSpec((tm, 2 * LANES), row)),
        compiler_params=_cparams(("parallel",)),
        name="cache_prep",
    )(ck, cv, cki)


def _pad_rows(a, rows):
    return jnp.pad(a, ((0, 0), (0, rows - a.shape[1]), (0, 0)))


def _attn_layer(x, g_mix, w_in, q_g, k_g, w_out, cache, *, tm, tq):
    b, t, _ = x.shape
    n = b * t
    past_len = 0 if cache is None else cache[0].shape[1]
    pos = past_len + jnp.arange(t, dtype=I32)
    reps = tm // t if tm > t else 1
    tabs = tuple(jnp.tile(a, (reps, 1)) for a in _rope_tables(pos, HEAD_DIM) + _rope_tables(pos, IDX_DIM))
    n_tab = max(t // tm, 1)
    src = _proj_column_sources()
    w_wide = jnp.take(jnp.pad(w_in.astype(BF16), ((0, 0), (0, 1))), src, axis=1)
    k_gp = jnp.take(jnp.pad(k_g, (0, ATTN_PROJ + 1 - HEAD_DIM)), _partner_columns(0, HEAD_DIM))
    gains = (q_g[None, :], k_g[None, :], k_gp[None, :])
    q, kf, vf, kif, kt, vb, kic, qit, wt = _attn_proj(
        x.reshape(n, D_MODEL), g_mix[None, :], w_wide, gains, tabs, tm=tm, n_tab=n_tab)
    kvw = N_KV_HEADS * HEAD_DIM
    wo = w_out.astype(BF16)
    if cache is None:
        y = _sparse_attn(x, q.reshape(b, t, O_Q), qit, wt, kic.reshape(b, t, kvw), kt, vb.reshape(b, t, kvw), wo,
                         tq=tq, pos0=0, l_true=t, n_valid=tq)
    else:
        ck, cv, cki = cache
        l_true = past_len + t
        lp = -(-l_true // KEY_TILE) * KEY_TILE
        n_new = lp - past_len
        kt_c, v_c, ki_c = _cache_prep(ck.reshape(b * past_len, N_KV_HEADS, HEAD_DIM),
                                      cv.reshape(b * past_len, N_KV_HEADS, HEAD_DIM),
                                      cki.reshape(b * past_len, IDX_DIM), tm=512)
        k_new = _pad_rows(kf.reshape(b, t, kvw).astype(BF16), n_new)
        kt_new = k_new.reshape(b, n_new // LANES, LANES, kvw).transpose(0, 1, 3, 2)
        kt_all = jnp.concatenate([kt_c.reshape(b, past_len // LANES, kvw, LANES), kt_new], 1)
        kt_all = kt_all.reshape(b * lp // LANES, kvw, LANES)
        v_all = jnp.concatenate([v_c.reshape(b, past_len, kvw), _pad_rows(vb.reshape(b, t, kvw), n_new)], 1)
        ki_all = jnp.concatenate([ki_c.reshape(b, past_len, kvw), _pad_rows(kic.reshape(b, t, kvw), n_new)], 1)
        q_p = _pad_rows(q.reshape(b, t, O_Q), tq)
        qit_p = jnp.pad(qit.reshape(O_Q, b, t), ((0, 0), (0, 0), (0, tq - t))).transpose(1, 0, 2)
        wt_p = jnp.pad(wt.reshape(N_IDX_HEADS, b, t), ((0, 0), (0, 0), (0, tq - t))).reshape(N_IDX_HEADS, b * tq)
        y = _sparse_attn(_pad_rows(x, tq), q_p, qit_p, wt_p, ki_all, kt_all, v_all, wo,
                         tq=tq, pos0=past_len, l_true=l_true, n_valid=t)[:, :t]
    return (y, kf.reshape(b, t, N_KV_HEADS, HEAD_DIM),
            vf.reshape(b, t, N_KV_HEADS, HEAD_DIM), kif.reshape(b, t, IDX_DIM))


def _gmlp_layer(x, g_mix, w_in, v_g, w_s, b_s, w_out, *, tm, emit_v):
    b, t, _ = x.shape
    n = b * t
    tc = min(t, GM_CHUNK)
    reps = GM_CHUNK // tc
    ws = jnp.tile(w_s[:, :tc, :tc], (1, reps, reps))
    bs = jnp.broadcast_to(jnp.tile(b_s[:, :tc], (1, reps))[:, :, None], (GM_GROUPS, GM_CHUNK, GM_CHUNK))
    y, v = _gmlp(x.reshape(n, D_MODEL), g_mix[None, :], w_in.astype(BF16), v_g[None, :], ws, bs,
                 w_out.astype(BF16), tm=tm, tc=tc, emit_v=emit_v)
    return y.reshape(b, t, D_MODEL), (v.reshape(b, t, D_MODEL) if emit_v else None)


def _ffn_ple_layer(x, p_all, layer, past, g_ffn, w_up, conv_w, conv_b, w_down, g_ple, w_gate, w_proj, *, tm):
    b, t, _ = x.shape
    n = b * t
    y, state = _conv_ffn_ple(x.reshape(n, D_MODEL), p_all.reshape(-1, p_all.shape[-1]), g_ffn[None, :],
                             w_up.astype(BF16), conv_w, conv_b[None, :], w_down.astype(BF16),
                             g_ple[None, :], w_gate.astype(BF16), w_proj.astype(BF16), past,
                             nb=b, tm=tm, layer=layer)
    return y.reshape(b, t, D_MODEL), state


def kernel(x_prompt, x_sample, cache_k, cache_v, cache_kidx, state_ffn_conv, p_prompt, p_sample,
           norm_mix, attn_w_in, attn_q_norm, attn_k_norm, attn_w_out,
           gmlp_w_in, gmlp_v_norm, gmlp_w_spatial, gmlp_b_spatial, gmlp_w_out,
           norm_ffn, ffn_w_up, ffn_conv_w, ffn_conv_b, ffn_w_down,
           norm_ple, ple_w_gate, ple_w_proj):
    depth = norm_mix.shape[0]
    t_s = x_sample.shape[1]
    n_s = x_sample.shape[0] * t_s
    yp, ys = x_prompt, x_sample
    kp, vp, kip, ks, vs, kis, gvs, cps, css = [], [], [], [], [], [], [], [], []
    for i in range(depth):
        j = i // 2
        if i % 2 == 0:
            yp, k, v, ki = _attn_layer(yp, norm_mix[i], attn_w_in[j], attn_q_norm[j], attn_k_norm[j],
                                       attn_w_out[j], None, tm=512, tq=256)
            kp.append(k); vp.append(v); kip.append(ki)
            ys, k, v, ki = _attn_layer(ys, norm_mix[i], attn_w_in[j], attn_q_norm[j], attn_k_norm[j],
                                       attn_w_out[j], (cache_k[j], cache_v[j], cache_kidx[j]), tm=n_s, tq=128)
            ks.append(k); vs.append(v); kis.append(ki)
        else:
            yp, _ = _gmlp_layer(yp, norm_mix[i], gmlp_w_in[j], gmlp_v_norm[j], gmlp_w_spatial[j],
                                gmlp_b_spatial[j], gmlp_w_out[j], tm=512, emit_v=False)
            ys, gv = _gmlp_layer(ys, norm_mix[i], gmlp_w_in[j], gmlp_v_norm[j], gmlp_w_spatial[j],
                                 gmlp_b_spatial[j], gmlp_w_out[j], tm=n_s, emit_v=True)
            gvs.append(gv)
        ffn = (norm_ffn[i], ffn_w_up[i], ffn_conv_w[i], ffn_conv_b[i], ffn_w_down[i],
               norm_ple[i], ple_w_gate[i], ple_w_proj[i])
        yp, cp = _ffn_ple_layer(yp, p_prompt, i, None, *ffn, tm=512)
        ys, cs = _ffn_ple_layer(ys, p_sample, i, state_ffn_conv[i], *ffn, tm=t_s)
        cps.append(cp); css.append(cs)
    return (yp, ys, jnp.stack(kp, 0), jnp.stack(vp, 0), jnp.stack(kip, 0),
            jnp.stack(ks, 0), jnp.stack(vs, 0), jnp.stack(kis, 0), jnp.stack(gvs, 0),
            jnp.stack(cps, 0), jnp.stack(css, 0))
```
